```python
import math
import jax, jax.numpy as jnp
from jax import lax
import numpy as np

D_MODEL = 1024
BATCH = 32
SEQ = 256
DEPTH = 4
DEC_BATCH = 8
DEC_SEQ = 2048
PAST_LEN = 256

GRID_W = 64
MIX_WIDTH = D_MODEL
HG_WIDTH = MIX_WIDTH // 2
HG_HEADS = 4
HG_DK = HG_WIDTH // HG_HEADS
HG_DV = HG_WIDTH // HG_HEADS
ML_WIDTH = MIX_WIDTH - HG_WIDTH
ML_HEADS = 4
ML_DK = ML_WIDTH // ML_HEADS
ML_DV = ML_WIDTH // ML_HEADS
N_DIR = 2
D_FF = -(-8 * D_MODEL // (3 * 256)) * 256
HG_CHUNK = 16
ML_CHUNK = 64
CONV_K = 3
EPS = 1e-6
D_IN_PROJ = 5 * HG_WIDTH + 4 * ML_WIDTH + 4 * ML_HEADS
SPLIT_POINTS = (HG_WIDTH, 2 * HG_WIDTH, 3 * HG_WIDTH, 4 * HG_WIDTH, 5 * HG_WIDTH,
                5 * HG_WIDTH + 2 * ML_WIDTH, 5 * HG_WIDTH + 3 * ML_WIDTH, 5 * HG_WIDTH + 4 * ML_WIDTH)

kernel_name = 'hymba_hgrn2_mlstm_flow_step'


def rmsnorm(x, w):
    x32 = x.astype(jnp.float32)
    y = x32 * lax.rsqrt(jnp.mean(x32 * x32, axis=-1, keepdims=True) + EPS)
    return (y * w).astype(x.dtype)


def head_rmsnorm(o, n_heads, w):
    b, l, d = o.shape
    o = o.reshape(b, l, n_heads, d // n_heads)
    o = o * lax.rsqrt(jnp.mean(o * o, axis=-1, keepdims=True) + EPS)
    return o.reshape(b, l, d) * w


def split_heads(t, n_heads):
    b, l, d = t.shape
    return t.reshape(b, l, n_heads, d // n_heads).transpose(0, 2, 1, 3)


def merge_heads(t):
    b, h, l, d = t.shape
    return t.transpose(0, 2, 1, 3).reshape(b, l, h * d)


def dwconv2d(x, taps, bias):
    ch = x.shape[-1]
    y = lax.conv_general_dilated(x, taps[:, :, None, :].astype(x.dtype), (1, 1), 'SAME',
                                 dimension_numbers=('NHWC', 'HWIO', 'NHWC'), feature_group_count=ch)
    return y + bias


def gla_chunk_scan(q, k, v, logf, s0):
    bsz, nh, seqlen, dk = q.shape
    dv = v.shape[-1]
    cs = HG_CHUNK
    nc = seqlen // cs
    qc = q.reshape(bsz, nh, nc, cs, dk)
    kc = k.reshape(bsz, nh, nc, cs, dk)
    vc = v.reshape(bsz, nh, nc, cs, dv)
    bcum = jnp.cumsum(logf.reshape(bsz, nh, nc, cs, dk), axis=3)
    causal = jnp.tril(jnp.ones((cs, cs), dtype=bool))[:, :, None]
    rel = bcum[:, :, :, :, None, :] - bcum[:, :, :, None, :, :]
    decay = jnp.exp(jnp.where(causal, rel, -jnp.inf))
    scores = jnp.einsum('bhntd,bhntsd,bhnsd->bhnts', qc, decay, kc)
    o_intra = jnp.einsum('bhnts,bhnsv->bhntv', scores, vc)
    b_last = bcum[:, :, :, -1, :]
    kv_chunk = jnp.einsum('bhnsd,bhnsv->bhndv', kc * jnp.exp(b_last[:, :, :, None, :] - bcum), vc)

    def step(s, inp):
        a_n, kv_n = inp
        return a_n[..., None] * s + kv_n, s

    s_final, s_start = lax.scan(step, s0, (jnp.moveaxis(jnp.exp(b_last), 2, 0),
                                           jnp.moveaxis(kv_chunk, 2, 0)))
    s_start = jnp.moveaxis(s_start, 0, 2)
    o_inter = jnp.einsum('bhntd,bhndv->bhntv', qc * jnp.exp(bcum), s_start)
    return (o_intra + o_inter).reshape(bsz, nh, seqlen, dv), s_final


def mlstm_chunk_scan(q, k, v, log_i, log_f, c0, n0, m0):
    bsz, nh, seqlen, dk = q.shape
    dv = v.shape[-1]
    cs = ML_CHUNK
    nc = seqlen // cs
    qc = q.reshape(bsz, nh, nc, cs, dk)
    kc = k.reshape(bsz, nh, nc, cs, dk)
    vc = v.reshape(bsz, nh, nc, cs, dv)
    ic = log_i.reshape(bsz, nh, nc, cs)
    bcum = jnp.cumsum(log_f.reshape(bsz, nh, nc, cs), axis=3)
    b_last = bcum[..., -1]
    w_end = b_last[..., None] - bcum + ic
    m_loc = jnp.max(w_end, axis=-1)
    p_end = jnp.exp(w_end - m_loc[..., None])
    kv_loc = jnp.einsum('bhns,bhnsd,bhnsv->bhndv', p_end, kc, vc)
    kn_loc = jnp.einsum('bhns,bhnsd->bhnd', p_end, kc)

    def step(carry, inp):
        c_s, n_s, m_s = carry
        bl, ml, kvl, knl = inp
        m_new = jnp.maximum(bl + m_s, ml)
        a = jnp.exp(bl + m_s - m_new)
        g = jnp.exp(ml - m_new)
        c_new = a[..., None, None] * c_s + g[..., None, None] * kvl
        n_new = a[..., None] * n_s + g[..., None] * knl
        return (c_new, n_new, m_new), (c_s, n_s, m_s)

    mv = lambda t: jnp.moveaxis(t, 2, 0)
    (c_f, n_f, m_f), (c_st, n_st, m_st) = lax.scan(
        step, (c0, n0, m0), (mv(b_last), mv(m_loc), mv(kv_loc), mv(kn_loc)))
    c_st = jnp.moveaxis(c_st, 0, 2)
    n_st = jnp.moveaxis(n_st, 0, 2)
    m_st = jnp.moveaxis(m_st, 0, 2)
    causal = jnp.tril(jnp.ones((cs, cs), dtype=bool))
    d_log = jnp.where(causal, bcum[..., :, None] - bcum[..., None, :] + ic[..., None, :], -jnp.inf)
    inter_log = bcum + m_st[..., None]
    m_t = jnp.maximum(inter_log, jnp.max(d_log, axis=-1))
    scores = jnp.einsum('bhntd,bhnsd->bhnts', qc, kc) * jnp.exp(d_log - m_t[..., None])
    a_inter = jnp.exp(inter_log - m_t)
    num = (jnp.einsum('bhnts,bhnsv->bhntv', scores, vc)
           + a_inter[..., None] * jnp.einsum('bhntd,bhndv->bhntv', qc, c_st))
    den = jnp.sum(scores, axis=-1) + a_inter * jnp.einsum('bhntd,bhnd->bhnt', qc, n_st)
    h = num / jnp.maximum(jnp.abs(den), jnp.exp(-m_t))[..., None]
    return h.reshape(bsz, nh, seqlen, dv), c_f, n_f, m_f


def run_direction(scan_fn, reverse, seq_args, state_args):
    if reverse:
        seq_args = [jnp.flip(a, axis=2) for a in seq_args]
    o, *fin = scan_fn(*seq_args, *state_args)
    if reverse:
        o = jnp.flip(o, axis=2)
    return o, fin


def token_mix(h, grid_hw, conv_taps, conv_b_l, states, lb_l, w_in_l, ml_gate_b_l,
              hg_norm_w_l, ml_norm_w_l, w_out_l):
    bsz, seqlen, _ = h.shape
    proj = jnp.einsum('bld,de->ble', h, w_in_l).astype(jnp.float32)
    hg_q, hg_ff, hg_fb, hg_i, hg_g, ml_qk, ml_v, ml_o, ml_gates = jnp.split(proj, SPLIT_POINTS, axis=-1)
    hg_s0, ml_c0, ml_n0, ml_m0 = [s.astype(jnp.float32) for s in states]

    q_h = split_heads(jax.nn.silu(hg_q), HG_HEADS)
    v_h = split_heads(hg_i, HG_HEADS)
    hg_outs, hg_fin = [], []
    for d, fz in enumerate((hg_ff, hg_fb)):
        lb = lb_l[d].astype(jnp.float32)
        logf = jnp.logaddexp(jnp.log(lb), jnp.log1p(-lb) + jax.nn.log_sigmoid(fz))
        k_h = (1.0 - lb) * jax.nn.sigmoid(-fz)
        o, fin = run_direction(gla_chunk_scan, d == 1,
                               (q_h, split_heads(k_h, HG_HEADS), v_h, split_heads(logf, HG_HEADS)),
                               (hg_s0[:, d],))
        hg_outs.append(o)
        hg_fin.append(fin[0])
    hg_out = head_rmsnorm(merge_heads(hg_outs[0] + hg_outs[1]), HG_HEADS, hg_norm_w_l) * jax.nn.silu(hg_g)

    rows, cols = grid_hw
    qk = dwconv2d(ml_qk.reshape(bsz, rows, cols, 2 * ML_WIDTH), conv_taps, conv_b_l)
    qk = jax.nn.silu(qk.reshape(bsz, seqlen, 2 * ML_WIDTH))
    mq, mk = jnp.split(qk, 2, axis=-1)
    q_m = split_heads(mq, ML_HEADS)
    k_m = split_heads(mk, ML_HEADS) * (ML_DK ** -0.5)
    v_m = split_heads(ml_v, ML_HEADS)
    gates = (ml_gates + ml_gate_b_l).reshape(bsz, seqlen, 4, ML_HEADS).transpose(0, 2, 3, 1)
    ml_outs, c_fin, n_fin, m_fin = [], [], [], []
    for d in range(N_DIR):
        o, fin = run_direction(mlstm_chunk_scan, d == 1,
                               (q_m, k_m, v_m, gates[:, d], jax.nn.log_sigmoid(gates[:, 2 + d])),
                               (ml_c0[:, d], ml_n0[:, d], ml_m0[:, d]))
        ml_outs.append(o)
        c_fin.append(fin[0])
        n_fin.append(fin[1])
        m_fin.append(fin[2])
    ml_out = head_rmsnorm(merge_heads(ml_outs[0] + ml_outs[1]), ML_HEADS, ml_norm_w_l) * jax.nn.sigmoid(ml_o)

    mix = jnp.concatenate([hg_out, ml_out], axis=-1).astype(h.dtype)
    out = jnp.einsum('ble,ed->bld', mix, w_out_l)
    new_states = (jnp.stack(hg_fin, axis=1), jnp.stack(c_fin, axis=1),
                  jnp.stack(n_fin, axis=1), jnp.stack(m_fin, axis=1))
    return out, new_states


def trunk_layer(x, cond, grid_hw, conv_taps, states, n1, n2, w_mod_l, b_mod_l, conv_b_l, lb_l,
                w_in_l, ml_gate_b_l, hg_norm_w_l, ml_norm_w_l, w_out_l, w_gate_l, w_up_l, w_down_l):
    mod = (jax.nn.silu(cond) @ w_mod_l + b_mod_l)[:, None, :]
    sh1, sc1, g1, sh2, sc2, g2 = jnp.split(mod, 6, axis=-1)
    h = rmsnorm(x, n1) * (1.0 + sc1) + sh1
    mix, new_states = token_mix(h, grid_hw, conv_taps, conv_b_l, states, lb_l, w_in_l, ml_gate_b_l,
                                hg_norm_w_l, ml_norm_w_l, w_out_l)
    x = x + g1 * mix
    h = rmsnorm(x, n2) * (1.0 + sc2) + sh2
    ffn = (jax.nn.silu(h @ w_gate_l) * (h @ w_up_l)) @ w_down_l
    x = x + g2 * ffn
    return x, new_states


def setup_inputs(seed: int = 0) -> dict:
    key = jax.random.key(seed)
    ks = jax.random.split(key, 32)
    nrm = lambda k, shape, s=1.0: s * jax.random.normal(k, shape, jnp.float32)
    x_prompt = nrm(ks[0], (BATCH, SEQ, D_MODEL))
    x_sample = nrm(ks[1], (DEC_BATCH, DEC_SEQ, D_MODEL))
    state_hgrn = nrm(ks[2], (DEC_BATCH, DEPTH, N_DIR, HG_HEADS, HG_DK, HG_DV), 0.5)
    state_mlstm_c = nrm(ks[3], (DEC_BATCH, DEPTH, N_DIR, ML_HEADS, ML_DK, ML_DV), 0.5)
    state_mlstm_n = nrm(ks[4], (DEC_BATCH, DEPTH, N_DIR, ML_HEADS, ML_DK), 0.5)
    state_mlstm_m = nrm(ks[5], (DEC_BATCH, DEPTH, N_DIR, ML_HEADS), 0.5)
    c = nrm(ks[6], (DEC_BATCH, D_MODEL))
    c_ctx = nrm(ks[7], (D_MODEL,))
    norm1_w = 1.0 + nrm(ks[8], (DEPTH, D_MODEL), 0.02)
    norm2_w = 1.0 + nrm(ks[9], (DEPTH, D_MODEL), 0.02)
    w_mod = nrm(ks[10], (DEPTH, D_MODEL, 6 * D_MODEL), 0.5 * D_MODEL ** -0.5)
    b_mod = nrm(ks[11], (DEPTH, 6 * D_MODEL), 0.02)
    w_in = nrm(ks[12], (DEPTH, D_MODEL, D_IN_PROJ), D_MODEL ** -0.5)
    conv_w = nrm(ks[13], (DEPTH, CONV_K, CONV_K, 2 * ML_WIDTH), 1.0 / CONV_K)
    conv_b = nrm(ks[14], (DEPTH, 2 * ML_WIDTH), 0.02)
    ml_gate_b = jnp.concatenate([nrm(ks[15], (DEPTH, 2 * ML_HEADS), 0.1),
                                 3.0 + 3.0 * jax.random.uniform(ks[16], (DEPTH, 2 * ML_HEADS), jnp.float32)],
                                axis=-1)
    hg_lb_logits = 1.0 + nrm(ks[17], (DEPTH, N_DIR, HG_WIDTH), 0.5)
    hg_norm_w = 1.0 + nrm(ks[18], (DEPTH, HG_WIDTH), 0.02)
    ml_norm_w = 1.0 + nrm(ks[19], (DEPTH, ML_WIDTH), 0.02)
    w_out = nrm(ks[20], (DEPTH, MIX_WIDTH, D_MODEL), MIX_WIDTH ** -0.5)
    w_gate = nrm(ks[21], (DEPTH, D_MODEL, D_FF), D_MODEL ** -0.5)
    w_up = nrm(ks[22], (DEPTH, D_MODEL, D_FF), D_MODEL ** -0.5)
    w_down = nrm(ks[23], (DEPTH, D_FF, D_MODEL), D_FF ** -0.5)
    final_norm_w = 1.0 + nrm(ks[24], (D_MODEL,), 0.02)
    return {'x_prompt': x_prompt, 'x_sample': x_sample, 'state_hgrn': state_hgrn,
            'state_mlstm_c': state_mlstm_c, 'state_mlstm_n': state_mlstm_n, 'state_mlstm_m': state_mlstm_m,
            'c': c, 'c_ctx': c_ctx, 'norm1_w': norm1_w, 'norm2_w': norm2_w, 'w_mod': w_mod, 'b_mod': b_mod,
            'w_in': w_in, 'conv_w': conv_w, 'conv_b': conv_b, 'ml_gate_b': ml_gate_b,
            'hg_lb_logits': hg_lb_logits, 'hg_norm_w': hg_norm_w, 'ml_norm_w': ml_norm_w, 'w_out': w_out,
            'w_gate': w_gate, 'w_up': w_up, 'w_down': w_down, 'final_norm_w': final_norm_w}


def reference(x_prompt, x_sample, state_hgrn, state_mlstm_c, state_mlstm_n, state_mlstm_m, c, c_ctx,
              norm1_w, norm2_w, w_mod, b_mod, w_in, conv_w, conv_b, ml_gate_b, hg_lb_logits,
              hg_norm_w, ml_norm_w, w_out, w_gate, w_up, w_down, final_norm_w):
    lb_all = jnp.cumsum(jax.nn.softmax(hg_lb_logits.astype(jnp.float32), axis=0), axis=0)
    lb_all = lb_all - lb_all[0]
    n_ctx_req = x_prompt.shape[0]
    ctx_grid = (1, x_prompt.shape[1])
    rows = x_sample.shape[1] // GRID_W
    lat_grid = (rows, GRID_W)
    zero_states = (jnp.zeros((n_ctx_req, N_DIR, HG_HEADS, HG_DK, HG_DV), jnp.float32),
                   jnp.zeros((n_ctx_req, N_DIR, ML_HEADS, ML_DK, ML_DV), jnp.float32),
                   jnp.zeros((n_ctx_req, N_DIR, ML_HEADS, ML_DK), jnp.float32),
                   jnp.zeros((n_ctx_req, N_DIR, ML_HEADS), jnp.float32))
    xp, xs = x_prompt, x_sample
    hg_st, mc_st, mn_st, mm_st = [], [], [], []
    for l in range(DEPTH):
        shared = (norm1_w[l], norm2_w[l], w_mod[l], b_mod[l], conv_b[l], lb_all[l], w_in[l], ml_gate_b[l],
                  hg_norm_w[l], ml_norm_w[l], w_out[l], w_gate[l], w_up[l], w_down[l])
        xp, st = trunk_layer(xp, c_ctx[None, :], ctx_grid, conv_w[l, 1:2], zero_states, *shared)
        hg_st.append(st[0])
        mc_st.append(st[1])
        mn_st.append(st[2])
        mm_st.append(st[3])
        cached = (state_hgrn[:, l], state_mlstm_c[:, l], state_mlstm_n[:, l], state_mlstm_m[:, l])
        xs, _ = trunk_layer(xs, c, lat_grid, conv_w[l], cached, *shared)
    y_prompt = rmsnorm(xp, final_norm_w)
    y_sample = rmsnorm(xs, final_norm_w)
    return (y_prompt, y_sample, jnp.stack(hg_st, axis=1), jnp.stack(mc_st, axis=1),
            jnp.stack(mn_st, axis=1), jnp.stack(mm_st, axis=1))
```

```python
import functools

import numpy as np
import jax
import jax.numpy as jnp
from jax import lax
from jax.experimental import pallas as pl
from jax.experimental.pallas import tpu as pltpu

F32 = jnp.float32
BF16 = jnp.bfloat16

D_MODEL = 1024
N_HEADS = 4
D_HEAD = 128
GROUP_WIDTH = N_HEADS * D_HEAD
D_FF = 2816
LATENT_GRID_W = 64
EPS = 1e-6
N_MOD_ROWS = 16
N_MAIN_COLS = 9 * GROUP_WIDTH
N_PROJ_COLS = N_MAIN_COLS + GROUP_WIDTH
SCAN_BLOCK = 128
N_LEVELS = 7

VMEM_LIMIT_BYTES = 56 * 1024 * 1024


def _params(*semantics):
    return pltpu.CompilerParams(dimension_semantics=semantics, vmem_limit_bytes=VMEM_LIMIT_BYTES)


def _silu(x):
    return x * jax.nn.sigmoid(x)


def _dot(a, b):
    return jnp.dot(a, b, preferred_element_type=F32)


def _dot_nt(a, b):
    return lax.dot_general(a, b, (((1,), (1,)), ((), ())), preferred_element_type=F32)


def _split_bf16(x):
    hi = x.astype(BF16)
    lo = (x - hi.astype(F32)).astype(BF16)
    return jnp.concatenate([hi, lo], axis=1)


def _hgrn_tables():
    c = SCAN_BLOCK
    t = np.arange(c)[:, None]
    u = np.arange(c)[None, :]
    sums = np.zeros((2, 2 + N_LEVELS, c, c), np.float32)
    masks = np.zeros((2, 1 + N_LEVELS, c, c), np.float32)
    sums[0, 0] = u <= t
    sums[0, 1] = u > t
    sums[1, 0] = u >= t
    sums[1, 1] = u < t
    masks[:, 0] = t == u
    for i in range(N_LEVELS):
        m = c >> (i + 1)
        upper_t = (t % (2 * m)) >= m
        upper_u = (u % (2 * m)) >= m
        same = (t // (2 * m)) == (u // (2 * m))
        r = (t // (2 * m)) * 2 * m + m - 1
        sums[0, 2 + i] = np.where(upper_t, (u > r) & (u <= t), (u > t) & (u <= r))
        masks[0, 1 + i] = same & upper_t & ~upper_u
        r = (t // (2 * m)) * 2 * m + m
        sums[1, 2 + i] = np.where(upper_t, (u >= r) & (u < t), (u >= t) & (u < r))
        masks[1, 1 + i] = same & ~upper_t & upper_u
    return (jnp.asarray(sums.reshape(2, (2 + N_LEVELS) * c, c), BF16), jnp.asarray(masks, F32))


def _mlstm_tables():
    c = SCAN_BLOCK
    t = np.arange(c)[:, None]
    u = np.arange(c)[None, :]
    tri = np.stack([u <= t, u >= t]).astype(np.float32)
    strict = np.stack([t > u, t < u]).astype(np.float32)
    return jnp.asarray(tri, BF16), jnp.asarray(strict, F32), jnp.asarray(np.eye(c, dtype=np.float32))


def _lower_bound_kernel(logit_ref, lb_ref):
    z = logit_ref[...]
    z = z - jnp.max(z, axis=0, keepdims=True)
    e = jnp.exp(z)
    p = e / jnp.sum(e, axis=0, keepdims=True)
    depth = z.shape[0]
    run = p[0:1]
    first = run
    for l in range(depth):
        if l > 0:
            run = run + p[l:l + 1]
        lb_ref[l:l + 1, :] = run - first


def _lower_bounds(hg_lb_logits):
    depth = hg_lb_logits.shape[0]
    flat = hg_lb_logits.astype(F32).reshape(depth, 2 * GROUP_WIDTH)
    lb = pl.pallas_call(
        _lower_bound_kernel,
        out_shape=jax.ShapeDtypeStruct(flat.shape, F32),
        name="hgrn_lower_bounds",
    )(flat)
    return lb.reshape(depth, 2, N_HEADS, 1, D_HEAD)


def _mod_kernel(cond_ref, w_ref, b_ref, o_ref):
    c = cond_ref[...]
    o_ref[0] = _dot(_silu(c).astype(BF16), w_ref[0].astype(BF16)) + b_ref[0]


def _modulation(cond, w_mod, b_mod):
    depth = w_mod.shape[0]
    n_out = w_mod.shape[2]
    tn = 1536
    return pl.pallas_call(
        _mod_kernel,
        grid=(depth, n_out // tn),
        in_specs=[
            pl.BlockSpec((N_MOD_ROWS, D_MODEL), lambda l, j: (0, 0)),
            pl.BlockSpec((1, D_MODEL, tn), lambda l, j: (l, 0, j)),
            pl.BlockSpec((1, 1, tn), lambda l, j: (l, 0, j)),
        ],
        out_specs=pl.BlockSpec((1, N_MOD_ROWS, tn), lambda l, j: (l, 0, j)),
        out_shape=jax.ShapeDtypeStruct((depth, N_MOD_ROWS, n_out), F32),
        compiler_params=_params("arbitrary", "arbitrary"),
        name="adaln_modulation",
    )(cond, w_mod, b_mod.reshape(depth, 1, n_out))


def _rmsnorm(x, w):
    return x * lax.rsqrt(jnp.mean(x * x, axis=-1, keepdims=True) + EPS) * w


def _inproj_kernel(x_ref, sc_ref, sh_ref, nw_ref, w_ref, o_ref):
    h = _rmsnorm(x_ref[...], nw_ref[...]) * (1.0 + sc_ref[...]) + sh_ref[...]
    hb = h.astype(BF16)
    for n0 in range(0, N_PROJ_COLS, GROUP_WIDTH):
        o_ref[:, n0:n0 + GROUP_WIDTH] = _dot(hb, w_ref[:, n0:n0 + GROUP_WIDTH])


def _mod_spec(which, row_of_tile):
    return pl.BlockSpec((None, None, 1, D_MODEL), lambda i, *_: (row_of_tile(i), which, 0, 0))


def _in_projection(x, mod_l, norm_w, w_in, row_of_tile, tm):
    n_tok = x.shape[0]
    return pl.pallas_call(
        _inproj_kernel,
        grid=(n_tok // tm,),
        in_specs=[
            pl.BlockSpec((tm, D_MODEL), lambda i: (i, 0)),
            _mod_spec(1, row_of_tile),
            _mod_spec(0, row_of_tile),
            pl.BlockSpec((1, D_MODEL), lambda i: (0, 0)),
            pl.BlockSpec((D_MODEL, N_PROJ_COLS), lambda i: (0, 0)),
        ],
        out_specs=pl.BlockSpec((tm, N_PROJ_COLS), lambda i: (i, 0)),
        out_shape=jax.ShapeDtypeStruct((n_tok, N_PROJ_COLS), F32),
        compiler_params=_params("arbitrary"),
        name="norm_in_projection",
    )(x, mod_l, mod_l, norm_w, w_in)


def _hgrn_kernel(*refs, seq_len, has_state, want_state):
    q_ref, ff_ref, fb_ref, v_ref, g_ref, lb_ref, nw_ref, sums_ref, masks_ref = refs[:9]
    pos = 9
    s0_ref = None
    if has_state:
        s0_ref = refs[pos]
        pos += 1
    o_ref = refs[pos]
    pos += 1
    sfin_ref = None
    if want_state:
        sfin_ref = refs[pos]
        pos += 1
    acc_ref = refs[pos]

    c = SCAN_BLOCK
    n_blocks = seq_len // c
    norm_w = nw_ref[...]

    for d in range(2):
        lb = lb_ref[d]
        log_lb = jnp.log(lb)
        log_1m_lb = jnp.log1p(-lb)
        one_m_lb = 1.0 - lb
        fz_ref = ff_ref if d == 0 else fb_ref
        total_row = c - 1 if d == 0 else 0

        def block(j, state_t, d=d, lb=lb, log_lb=log_lb, log_1m_lb=log_1m_lb,
                  one_m_lb=one_m_lb, fz_ref=fz_ref, total_row=total_row):
            blk = j if d == 0 else n_blocks - 1 - j
            rows = pl.ds(pl.multiple_of(blk * c, c), c)
            q = _silu(q_ref[rows, :])
            v = v_ref[rows, :]
            fz = fz_ref[rows, :]
            e = jnp.exp(-jnp.abs(fz))
            log_sig = jnp.minimum(fz, 0.0) - jnp.log1p(e)
            term = log_1m_lb + log_sig
            logf = jnp.maximum(log_lb, term) + jnp.log1p(jnp.exp(-jnp.abs(log_lb - term)))
            k = one_m_lb * (jnp.where(fz >= 0.0, e, 1.0) / (1.0 + e))

            sums2 = _dot(sums_ref[d], _split_bf16(logf))
            sums = sums2[:, :D_HEAD] + sums2[:, D_HEAD:]
            decay_in = jnp.exp(sums[0:c])
            decay_out = jnp.exp(sums[c:2 * c])
            total = decay_in[total_row:total_row + 1]

            scores = _dot_nt(q.astype(BF16), k.astype(BF16)) * masks_ref[d, 0]
            for i in range(N_LEVELS):
                e_lvl = jnp.exp(sums[(2 + i) * c:(3 + i) * c])
                s_lvl = _dot_nt((q * e_lvl).astype(BF16), (k * e_lvl).astype(BF16))
                scores = scores + s_lvl * masks_ref[d, 1 + i]

            vb = v.astype(BF16)
            out = _dot(scores.astype(BF16), vb)
            out = out + _dot_nt((q * decay_in).astype(BF16), state_t.astype(BF16))
            new_state_t = state_t * total + _dot(v.T.astype(BF16), (k * decay_out).astype(BF16))

            if d == 0:
                acc_ref[rows, :] = out
            else:
                o = acc_ref[rows, :] + out
                o = o * lax.rsqrt(jnp.mean(o * o, axis=-1, keepdims=True) + EPS) * norm_w
                o_ref[rows, :] = (o * _silu(g_ref[rows, :])).astype(o_ref.dtype)
            return new_state_t

        if has_state:
            init = s0_ref[d].T
        else:
            init = jnp.zeros((D_HEAD, D_HEAD), F32)
        final = lax.fori_loop(0, n_blocks, block, init)
        if want_state:
            sfin_ref[d] = final.T


def _hgrn_mixer(proj, lb_l, norm_w, state0, batch, seq_len, want_state, tables):
    sums, masks = tables
    has_state = state0 is not None
    n_tok = batch * seq_len

    def col(section):
        return pl.BlockSpec((seq_len, D_HEAD), lambda b, h: (b, section * N_HEADS + h))

    in_specs = [col(0), col(1), col(2), col(3), col(4),
                pl.BlockSpec((2, None, 1, D_HEAD), lambda b, h: (0, h, 0, 0)),
                pl.BlockSpec((None, 1, D_HEAD), lambda b, h: (h, 0, 0)),
                pl.BlockSpec(sums.shape, lambda b, h: (0, 0, 0)),
                pl.BlockSpec(masks.shape, lambda b, h: (0, 0, 0, 0))]
    args = [proj, proj, proj, proj, proj, lb_l, norm_w, sums, masks]
    state_spec = pl.BlockSpec((None, 2, None, D_HEAD, D_HEAD), lambda b, h: (b, 0, h, 0, 0))
    if has_state:
        in_specs.append(state_spec)
        args.append(state0)
    out_shape = [jax.ShapeDtypeStruct((n_tok, GROUP_WIDTH), BF16)]
    out_specs = [pl.BlockSpec((seq_len, D_HEAD), lambda b, h: (b, h))]
    if want_state:
        out_shape.append(jax.ShapeDtypeStruct((batch, 2, N_HEADS, D_HEAD, D_HEAD), F32))
        out_specs.append(state_spec)
    res = pl.pallas_call(
        functools.partial(_hgrn_kernel, seq_len=seq_len, has_state=has_state, want_state=want_state),
        grid=(batch, N_HEADS),
        in_specs=in_specs,
        out_specs=out_specs,
        out_shape=out_shape,
        scratch_shapes=[pltpu.VMEM((seq_len, D_HEAD), F32)],
        compiler_params=_params("arbitrary", "arbitrary"),
        name="hgrn2_mixer",
    )(*args)
    return res if want_state else (res[0], None)


def _dwconv(x, taps, bias, grid_w):
    n = x.shape[0]
    col = lax.broadcasted_iota(jnp.int32, (n, 1), 0) % grid_w
    left = jnp.where(col == 0, 0.0, pltpu.roll(x, 1, 0))
    right = jnp.where(col == grid_w - 1, 0.0, pltpu.roll(x, n - 1, 0))

    def kernel_row(i):
        return left * taps[3 * i] + x * taps[3 * i + 1] + right * taps[3 * i + 2]

    y = kernel_row(1) + bias
    if grid_w < n:
        pad = jnp.zeros((grid_w, x.shape[1]), x.dtype)
        y = y + jnp.concatenate([pad, kernel_row(0)[:n - grid_w]], axis=0)
        y = y + jnp.concatenate([kernel_row(2)[grid_w:], pad], axis=0)
    return y


def _mlstm_kernel(*refs, seq_len, grid_w, has_state, want_state):
    (q_ref, k_ref, v_ref, og_ref, gates_ref, qtap_ref, ktap_ref, qb_ref, kb_ref, gb_ref, nw_ref,
     tri_ref, strict_ref, eye_ref) = refs[:14]
    pos = 14
    c0_ref = n0_ref = m0_ref = None
    if has_state:
        c0_ref, n0_ref, m0_ref = refs[pos:pos + 3]
        pos += 3
    o_ref = refs[pos]
    pos += 1
    cfin_ref = nfin_ref = mfin_ref = None
    if want_state:
        cfin_ref, nfin_ref, mfin_ref = refs[pos:pos + 3]
        pos += 3
    qc_ref, kc_ref, acc_ref = refs[pos:pos + 3]

    c = SCAN_BLOCK
    n_blocks = seq_len // c
    qc_ref[...] = _silu(_dwconv(q_ref[...], qtap_ref[...], qb_ref[...], grid_w))
    kc_ref[...] = _silu(_dwconv(k_ref[...], ktap_ref[...], kb_ref[...], grid_w)) * (D_HEAD ** -0.5)

    gate_bias = gb_ref[...]
    norm_w = nw_ref[...]
    lane = lax.broadcasted_iota(jnp.int32, (1, D_HEAD), 1)
    ones_col = jnp.broadcast_to(jnp.where(lane == 0, 1.0, 0.0), (c, D_HEAD))

    for d in range(2):
        last_row = c - 1 if d == 0 else 0

        def block(j, carry, d=d, last_row=last_row):
            state, m_run = carry
            blk = j if d == 0 else n_blocks - 1 - j
            rows = pl.ds(pl.multiple_of(blk * c, c), c)
            q = qc_ref[rows, :]
            k = kc_ref[rows, :]
            v = v_ref[rows, :]
            gates = gates_ref[rows, :] + gate_bias
            log_i = gates[:, d:d + 1]
            fz = gates[:, 2 + d:3 + d]
            log_f = jnp.minimum(fz, 0.0) - jnp.log1p(jnp.exp(-jnp.abs(fz)))

            tri = tri_ref[d]
            vis = (strict_ref[d] + eye_ref[...]) > 0.5
            mix = log_f * strict_ref[d] + log_i * eye_ref[...]
            d2 = _dot(tri, _split_bf16(mix))
            d_log = d2[:, :c] + d2[:, c:]
            b2 = _dot(tri, _split_bf16(jnp.broadcast_to(log_f, (c, D_HEAD))))
            b_run = b2[:, :D_HEAD] + b2[:, D_HEAD:]

            inter_log = b_run + m_run
            d_max = jnp.max(jnp.where(vis, d_log, -jnp.inf), axis=1, keepdims=True)
            m_t = jnp.maximum(inter_log[:, 0:1], d_max)
            p = jnp.where(vis, jnp.exp(d_log - m_t), 0.0)
            scores = _dot_nt(q.astype(BF16), k.astype(BF16)) * p
            v_aug = jnp.concatenate([v, ones_col], axis=1).astype(BF16)
            intra = _dot(scores.astype(BF16), v_aug)
            inter = _dot(q.astype(BF16), state.astype(BF16))
            a_inter = jnp.exp(inter_log[:, 0:1] - m_t)
            num = intra[:, :D_HEAD] + a_inter * inter[:, :D_HEAD]
            den = intra[:, D_HEAD:D_HEAD + 1] + a_inter * inter[:, D_HEAD:D_HEAD + 1]
            out = num / jnp.maximum(jnp.abs(den), jnp.exp(-m_t))

            b_last = b_run[last_row:last_row + 1]
            w_end = b_last - b_run + log_i
            m_loc = jnp.max(w_end, axis=0, keepdims=True)
            p_end = jnp.exp(w_end - m_loc)
            kv = _dot((k * p_end).T.astype(BF16), v_aug)
            m_new = jnp.maximum(b_last + m_run, m_loc)
            a = jnp.exp(b_last + m_run - m_new)
            g = jnp.exp(m_loc - m_new)
            new_state = a[:, 0:1] * state + g[:, 0:1] * kv

            if d == 0:
                acc_ref[rows, :] = out
            else:
                o = acc_ref[rows, :] + out
                o = o * lax.rsqrt(jnp.mean(o * o, axis=-1, keepdims=True) + EPS) * norm_w
                o_ref[rows, :] = (o * jax.nn.sigmoid(og_ref[rows, :])).astype(o_ref.dtype)
            return new_state, m_new

        if has_state:
            n_col = jnp.where(lane == 0, n0_ref[d], 0.0)
            init = (jnp.concatenate([c0_ref[d], n_col], axis=1), m0_ref[d])
        else:
            init = (jnp.zeros((D_HEAD, 2 * D_HEAD), F32), jnp.zeros((1, D_HEAD), F32))
        state, m_fin = lax.fori_loop(0, n_blocks, block, init)
        if want_state:
            cfin_ref[d] = state[:, :D_HEAD]
            nfin_ref[d] = state[:, D_HEAD:D_HEAD + 1]
            mfin_ref[d] = m_fin


def _mlstm_mixer(proj, conv_taps, conv_bias, gate_bias, norm_w, states0, batch, seq_len, grid_w,
                 want_state, tables):
    tri, strict, eye = tables
    has_state = states0 is not None
    n_tok = batch * seq_len

    def col(block_index):
        return pl.BlockSpec((seq_len, D_HEAD), lambda b, h: (b, block_index(h)))

    def const(arr):
        return pl.BlockSpec(arr.shape, lambda b, h: (0,) * arr.ndim)

    def per_head(offset):
        return pl.BlockSpec((None, 1, D_HEAD), lambda b, h: (offset + h, 0, 0))

    in_specs = [col(lambda h: 5 * N_HEADS + h), col(lambda h: 6 * N_HEADS + h),
                col(lambda h: 7 * N_HEADS + h), col(lambda h: 8 * N_HEADS + h),
                col(lambda h: 9 * N_HEADS + h),
                pl.BlockSpec((9, None, 1, D_HEAD), lambda b, h: (0, h, 0, 0)),
                pl.BlockSpec((9, None, 1, D_HEAD), lambda b, h: (0, N_HEADS + h, 0, 0)),
                per_head(0), per_head(N_HEADS), per_head(0), per_head(0),
                const(tri), const(strict), const(eye)]
    args = [proj, proj, proj, proj, proj, conv_taps, conv_taps, conv_bias, conv_bias, gate_bias,
            norm_w, tri, strict, eye]

    def state_spec(rows, lanes):
        return pl.BlockSpec((None, 2, None, rows, lanes), lambda b, h: (b, 0, h, 0, 0))

    state_specs = [state_spec(D_HEAD, D_HEAD), state_spec(D_HEAD, 1), state_spec(1, D_HEAD)]
    if has_state:
        in_specs += state_specs
        args += list(states0)
    out_shape = [jax.ShapeDtypeStruct((n_tok, GROUP_WIDTH), BF16)]
    out_specs = [pl.BlockSpec((seq_len, D_HEAD), lambda b, h: (b, h))]
    if want_state:
        out_shape += [jax.ShapeDtypeStruct((batch, 2, N_HEADS, D_HEAD, D_HEAD), F32),
                      jax.ShapeDtypeStruct((batch, 2, N_HEADS, D_HEAD, 1), F32),
                      jax.ShapeDtypeStruct((batch, 2, N_HEADS, 1, D_HEAD), F32)]
        out_specs += state_specs
    res = pl.pallas_call(
        functools.partial(_mlstm_kernel, seq_len=seq_len, grid_w=grid_w, has_state=has_state,
                          want_state=want_state),
        grid=(batch, N_HEADS),
        in_specs=in_specs,
        out_specs=out_specs,
        out_shape=out_shape,
        scratch_shapes=[pltpu.VMEM((seq_len, D_HEAD), F32)] * 3,
        compiler_params=_params("arbitrary", "arbitrary"),
        name="mlstm_mixer",
    )(*args)
    return (res[0], tuple(res[1:])) if want_state else (res[0], None)


def _post_kernel(x_ref, mh_ref, mm_ref, g1_ref, sc2_ref, sh2_ref, g2_ref, n2_ref, woh_ref, wom_ref,
                 wg_ref, wu_ref, wd_ref, fn_ref, o_ref, x1_ref, h2_ref, acc_ref, *, final_norm):
    j = pl.program_id(1)

    @pl.when(j == 0)
    def _():
        mixed = _dot(mh_ref[...], woh_ref[...]) + _dot(mm_ref[...], wom_ref[...])
        x1 = x_ref[...] + g1_ref[...] * mixed
        x1_ref[...] = x1
        h2 = _rmsnorm(x1, n2_ref[...]) * (1.0 + sc2_ref[...]) + sh2_ref[...]
        h2_ref[...] = h2.astype(BF16)
        acc_ref[...] = jnp.zeros_like(acc_ref)

    hb = h2_ref[...]
    act = _silu(_dot(hb, wg_ref[...])) * _dot(hb, wu_ref[...])
    acc_ref[...] += _dot(act.astype(BF16), wd_ref[...])

    @pl.when(j == pl.num_programs(1) - 1)
    def _():
        x2 = x1_ref[...] + g2_ref[...] * acc_ref[...]
        if final_norm:
            x2 = _rmsnorm(x2, fn_ref[...])
        o_ref[...] = x2


def _post_mixer(x, mix_h, mix_m, mod_l, norm2_w, w_out_h, w_out_m, w_gate, w_up, w_down,
                final_norm_w, row_of_tile, tm, final_norm):
    n_tok = x.shape[0]
    tf = D_FF // 2
    row = lambda i, j: (i, 0)
    fixed = lambda i, j: (0, 0)
    return pl.pallas_call(
        functools.partial(_post_kernel, final_norm=final_norm),
        grid=(n_tok // tm, D_FF // tf),
        in_specs=[
            pl.BlockSpec((tm, D_MODEL), row),
            pl.BlockSpec((tm, GROUP_WIDTH), row),
            pl.BlockSpec((tm, GROUP_WIDTH), row),
            _mod_spec(2, row_of_tile), _mod_spec(4, row_of_tile), _mod_spec(3, row_of_tile),
            _mod_spec(5, row_of_tile),
            pl.BlockSpec((1, D_MODEL), fixed),
            pl.BlockSpec((GROUP_WIDTH, D_MODEL), fixed),
            pl.BlockSpec((GROUP_WIDTH, D_MODEL), fixed),
            pl.BlockSpec((D_MODEL, tf), lambda i, j: (0, j)),
            pl.BlockSpec((D_MODEL, tf), lambda i, j: (0, j)),
            pl.BlockSpec((tf, D_MODEL), lambda i, j: (j, 0)),
            pl.BlockSpec((1, D_MODEL), fixed),
        ],
        out_specs=pl.BlockSpec((tm, D_MODEL), row),
        out_shape=jax.ShapeDtypeStruct((n_tok, D_MODEL), F32),
        scratch_shapes=[pltpu.VMEM((tm, D_MODEL), F32), pltpu.VMEM((tm, D_MODEL), BF16),
                        pltpu.VMEM((tm, D_MODEL), F32)],
        compiler_params=_params("arbitrary", "arbitrary"),
        name="out_projection_ffn",
    )(x, mix_h, mix_m, mod_l, mod_l, mod_l, mod_l, norm2_w, w_out_h, w_out_m, w_gate, w_up, w_down,
      final_norm_w)


def _pack_in_projection(w_in_l):
    main = w_in_l[:, :N_MAIN_COLS]
    gates = w_in_l[:, N_MAIN_COLS:].reshape(D_MODEL, 4, N_HEADS).transpose(0, 2, 1)
    gates = jnp.pad(gates, ((0, 0), (0, 0), (0, D_HEAD - 4))).reshape(D_MODEL, GROUP_WIDTH)
    return jnp.concatenate([main, gates], axis=1).astype(BF16)


def kernel(x_prompt, x_sample, state_hgrn, state_mlstm_c, state_mlstm_n, state_mlstm_m, c, c_ctx,
           norm1_w, norm2_w, w_mod, b_mod, w_in, conv_w, conv_b, ml_gate_b, hg_lb_logits,
           hg_norm_w, ml_norm_w, w_out, w_gate, w_up, w_down, final_norm_w):
    depth = w_in.shape[0]
    n_ctx, ctx_len, _ = x_prompt.shape
    n_lat, lat_len, _ = x_sample.shape
    assert 1 + n_lat <= N_MOD_ROWS and ctx_len % SCAN_BLOCK == 0 and lat_len % SCAN_BLOCK == 0

    hg_tables = _hgrn_tables()
    ml_tables = _mlstm_tables()
    lb_all = _lower_bounds(hg_lb_logits)
    cond = jnp.zeros((N_MOD_ROWS, D_MODEL), F32).at[0].set(c_ctx).at[1:1 + n_lat].set(c)
    mod = _modulation(cond, w_mod, b_mod).reshape(depth, N_MOD_ROWS, 6, 1, D_MODEL)

    tm_in, tm_post = 256, 512
    ctx_row_in = ctx_row_post = lambda i: 0
    lat_row_in = lambda i: 1 + i // (lat_len // tm_in)
    lat_row_post = lambda i: 1 + i // (lat_len // tm_post)

    xp = x_prompt.reshape(n_ctx * ctx_len, D_MODEL)
    xs = x_sample.reshape(n_lat * lat_len, D_MODEL)
    fin_w = final_norm_w.reshape(1, D_MODEL)
    hg_fin, mc_fin, mn_fin, mm_fin = [], [], [], []
    for l in range(depth):
        w_in_l = _pack_in_projection(w_in[l])
        n1 = norm1_w[l].reshape(1, D_MODEL)
        n2 = norm2_w[l].reshape(1, D_MODEL)
        w_out_h = w_out[l, :GROUP_WIDTH].astype(BF16)
        w_out_m = w_out[l, GROUP_WIDTH:].astype(BF16)
        wg, wu, wd = w_gate[l].astype(BF16), w_up[l].astype(BF16), w_down[l].astype(BF16)
        taps = conv_w[l].reshape(9, 2 * N_HEADS, 1, D_HEAD)
        cbias = conv_b[l].reshape(2 * N_HEADS, 1, D_HEAD)
        gbias = jnp.pad(ml_gate_b[l].reshape(4, N_HEADS).T, ((0, 0), (0, D_HEAD - 4)))
        gbias = gbias.reshape(N_HEADS, 1, D_HEAD)
        hg_nw = hg_norm_w[l].reshape(N_HEADS, 1, D_HEAD)
        ml_nw = ml_norm_w[l].reshape(N_HEADS, 1, D_HEAD)
        last = l == depth - 1

        proj = _in_projection(xp, mod[l], n1, w_in_l, ctx_row_in, tm_in)
        mix_h, s_h = _hgrn_mixer(proj, lb_all[l], hg_nw, None, n_ctx, ctx_len, True, hg_tables)
        mix_m, s_m = _mlstm_mixer(proj, taps, cbias, gbias, ml_nw, None, n_ctx, ctx_len, ctx_len,
                                  True, ml_tables)
        xp = _post_mixer(xp, mix_h, mix_m, mod[l], n2, w_out_h, w_out_m, wg, wu, wd, fin_w,
                         ctx_row_post, tm_post, last)
        hg_fin.append(s_h)
        mc_fin.append(s_m[0])
        mn_fin.append(s_m[1][..., 0])
        mm_fin.append(s_m[2][..., 0, 0])

        cached = (state_mlstm_c[:, l].astype(F32), state_mlstm_n[:, l].astype(F32)[..., None],
                  jnp.broadcast_to(state_mlstm_m[:, l].astype(F32)[..., None, None],
                                   (n_lat, 2, N_HEADS, 1, D_HEAD)))
        proj = _in_projection(xs, mod[l], n1, w_in_l, lat_row_in, tm_in)
        mix_h, _ = _hgrn_mixer(proj, lb_all[l], hg_nw, state_hgrn[:, l].astype(F32), n_lat, lat_len,
                               False, hg_tables)
        mix_m, _ = _mlstm_mixer(proj, taps, cbias, gbias, ml_nw, cached, n_lat, lat_len,
                                LATENT_GRID_W, False, ml_tables)
        xs = _post_mixer(xs, mix_h, mix_m, mod[l], n2, w_out_h, w_out_m, wg, wu, wd, fin_w,
                         lat_row_post, tm_post, last)

    return (xp.reshape(x_prompt.shape), xs.reshape(x_sample.shape), jnp.stack(hg_fin, axis=1),
            jnp.stack(mc_fin, axis=1), jnp.stack(mn_fin, axis=1), jnp.stack(mm_fin, axis=1))
```

```python
import functools

import numpy as np
import jax
import jax.numpy as jnp
from jax import lax
from jax.experimental import pallas as pl
from jax.experimental.pallas import tpu as pltpu

F32 = jnp.float32
BF16 = jnp.bfloat16

D_MODEL = 1024
N_HEADS = 4
D_HEAD = 128
GROUP_WIDTH = N_HEADS * D_HEAD
D_FF = 2816
LATENT_GRID_W = 64
EPS = 1e-6
N_MOD_ROWS = 16
N_MAIN_COLS = 9 * GROUP_WIDTH
N_PROJ_COLS = N_MAIN_COLS + GROUP_WIDTH
SCAN_BLOCK = 128
N_LEVELS = 7

VMEM_LIMIT_BYTES = 56 * 1024 * 1024


def _params(*semantics):
    return pltpu.CompilerParams(dimension_semantics=semantics, vmem_limit_bytes=VMEM_LIMIT_BYTES)


def _sigmoid(x):
    return 1.0 / (1.0 + jnp.exp(-x))


def _silu(x):
    return x * _sigmoid(x)


def _log_sigmoid(x):
    return jnp.minimum(x, 0.0) - jnp.log(1.0 + jnp.exp(-jnp.abs(x)))


def _dot(a, b):
    return jnp.dot(a, b, preferred_element_type=F32)


def _dot_nt(a, b):
    return lax.dot_general(a, b, (((1,), (1,)), ((), ())), preferred_element_type=F32)


def _split_bf16(x):
    hi = x.astype(BF16)
    lo = (x - hi.astype(F32)).astype(BF16)
    return jnp.concatenate([hi, lo], axis=1)


def _block_rows(blk):
    return pl.ds(pl.multiple_of(blk * SCAN_BLOCK, SCAN_BLOCK), SCAN_BLOCK)


def _hgrn_tables():
    c = SCAN_BLOCK
    t = np.arange(c)[:, None]
    u = np.arange(c)[None, :]
    scan = np.stack([u <= t, u >= t]).astype(np.float32)
    masks = np.zeros((2, 1 + N_LEVELS, c, c), np.float32)
    masks[:, 0] = t == u
    for i in range(N_LEVELS):
        m = c >> (i + 1)
        upper_t = (t % (2 * m)) >= m
        upper_u = (u % (2 * m)) >= m
        same = (t // (2 * m)) == (u // (2 * m))
        masks[0, 1 + i] = same & upper_t & ~upper_u
        masks[1, 1 + i] = same & ~upper_t & upper_u
    return jnp.asarray(scan, BF16), jnp.asarray(masks, F32)


def _mlstm_tables():
    c = SCAN_BLOCK
    t = np.arange(c)[:, None]
    u = np.arange(c)[None, :]
    tri = np.stack([u <= t, u >= t]).astype(np.float32)
    strict = np.stack([t > u, t < u]).astype(np.float32)
    return jnp.asarray(tri, BF16), jnp.asarray(strict, F32), jnp.asarray(np.eye(c, dtype=np.float32))


def _lower_bound_kernel(logit_ref, lb_ref):
    z = logit_ref[...]
    z = z - jnp.max(z, axis=0, keepdims=True)
    e = jnp.exp(z)
    p = e / jnp.sum(e, axis=0, keepdims=True)
    depth = z.shape[0]
    run = p[0:1]
    first = run
    for l in range(depth):
        if l > 0:
            run = run + p[l:l + 1]
        lb_ref[l:l + 1, :] = run - first


def _lower_bounds(hg_lb_logits):
    depth = hg_lb_logits.shape[0]
    flat = hg_lb_logits.astype(F32).reshape(depth, 2 * GROUP_WIDTH)
    lb = pl.pallas_call(
        _lower_bound_kernel,
        out_shape=jax.ShapeDtypeStruct(flat.shape, F32),
        name="hgrn_lower_bounds",
    )(flat)
    return lb.reshape(depth, 2, N_HEADS, 1, D_HEAD)


def _mod_kernel(cond_ref, w_ref, b_ref, o_ref):
    c = cond_ref[...]
    o_ref[0] = _dot(_silu(c).astype(BF16), w_ref[0].astype(BF16)) + b_ref[0]


def _modulation(cond, w_mod, b_mod):
    depth = w_mod.shape[0]
    n_out = w_mod.shape[2]
    tn = 1536
    return pl.pallas_call(
        _mod_kernel,
        grid=(depth, n_out // tn),
        in_specs=[
            pl.BlockSpec((N_MOD_ROWS, D_MODEL), lambda l, j: (0, 0)),
            pl.BlockSpec((1, D_MODEL, tn), lambda l, j: (l, 0, j)),
            pl.BlockSpec((1, 1, tn), lambda l, j: (l, 0, j)),
        ],
        out_specs=pl.BlockSpec((1, N_MOD_ROWS, tn), lambda l, j: (l, 0, j)),
        out_shape=jax.ShapeDtypeStruct((depth, N_MOD_ROWS, n_out), F32),
        compiler_params=_params("arbitrary", "arbitrary"),
        name="adaln_modulation",
    )(cond, w_mod, b_mod.reshape(depth, 1, n_out))


def _rmsnorm(x, w):
    return x * lax.rsqrt(jnp.mean(x * x, axis=-1, keepdims=True) + EPS) * w


def _inproj_kernel(x_ref, sc_ref, sh_ref, nw_ref, w_ref, o_ref):
    h = _rmsnorm(x_ref[...], nw_ref[...]) * (1.0 + sc_ref[...]) + sh_ref[...]
    hb = h.astype(BF16)
    for n0 in range(0, N_PROJ_COLS, GROUP_WIDTH):
        o_ref[:, n0:n0 + GROUP_WIDTH] = _dot(hb, w_ref[:, n0:n0 + GROUP_WIDTH])


def _mod_spec(which, row_of_tile):
    return pl.BlockSpec((None, None, 1, D_MODEL), lambda i, *_: (row_of_tile(i), which, 0, 0))


def _in_projection(x, mod_l, norm_w, w_in, row_of_tile, tm):
    n_tok = x.shape[0]
    return pl.pallas_call(
        _inproj_kernel,
        grid=(n_tok // tm,),
        in_specs=[
            pl.BlockSpec((tm, D_MODEL), lambda i: (i, 0)),
            _mod_spec(1, row_of_tile),
            _mod_spec(0, row_of_tile),
            pl.BlockSpec((1, D_MODEL), lambda i: (0, 0)),
            pl.BlockSpec((D_MODEL, N_PROJ_COLS), lambda i: (0, 0)),
        ],
        out_specs=pl.BlockSpec((tm, N_PROJ_COLS), lambda i: (i, 0)),
        out_shape=jax.ShapeDtypeStruct((n_tok, N_PROJ_COLS), F32),
        compiler_params=_params("arbitrary"),
        name="norm_in_projection",
    )(x, mod_l, mod_l, norm_w, w_in)


def _hgrn_block(d, rows, state_t, q_ref, fz_ref, v_ref, lb, scan_ref, masks_ref, b_scr):
    c = SCAN_BLOCK
    q = _silu(q_ref[rows, :])
    v = v_ref[rows, :]
    fz = fz_ref[rows, :]
    e = jnp.exp(-jnp.abs(fz))
    inv = 1.0 / (1.0 + e)
    pos = fz >= 0.0
    k = (1.0 - lb) * (jnp.where(pos, e, 1.0) * inv)
    f = lb + (1.0 - lb) * (jnp.where(pos, 1.0, e) * inv)
    logf = jnp.where(lb > 0.0, jnp.log(f), jnp.minimum(fz, 0.0) - jnp.log(1.0 + e))

    b2 = _dot(scan_ref[d], _split_bf16(logf))
    b = b2[:, :D_HEAD] + b2[:, D_HEAD:]
    total_row = c - 1 if d == 0 else 0
    total = b[total_row:total_row + 1]
    decay_in = jnp.exp(b)
    decay_out = jnp.exp(total - b)

    b_scr[...] = b
    scores = _dot_nt(q.astype(BF16), k.astype(BF16)) * masks_ref[d, 0]
    for i in range(N_LEVELS):
        m = c >> (i + 1)
        if m >= 4:
            parts = []
            for g in range(c // (2 * m)):
                r = g * 2 * m + (m - 1 if d == 0 else m)
                seg = b[g * 2 * m:(g + 1) * 2 * m]
                parts.append(jnp.exp(-jnp.abs(seg - b_scr[r:r + 1, :])))
            e_lvl = parts[0] if len(parts) == 1 else jnp.concatenate(parts, axis=0)
        else:
            t_idx = lax.broadcasted_iota(jnp.int32, (c, D_HEAD), 0)
            f_prev = pltpu.roll(f, 1, 0)
            f_next = pltpu.roll(f, c - 1, 0)
            if m == 2:
                ph = t_idx % 4
                if d == 0:
                    e_lvl = jnp.where(ph == 0, f_next, jnp.where(ph == 1, 1.0,
                                      jnp.where(ph == 2, f, f * f_prev)))
                else:
                    e_lvl = jnp.where(ph == 0, f * f_next, jnp.where(ph == 1, f,
                                      jnp.where(ph == 2, 1.0, f_prev)))
            else:
                e_lvl = jnp.where(t_idx % 2 == (1 - d), f, 1.0)
        s_lvl = _dot_nt((q * e_lvl).astype(BF16), (k * e_lvl).astype(BF16))
        scores = scores + s_lvl * masks_ref[d, 1 + i]

    out = _dot(scores.astype(BF16), v.astype(BF16))
    out = out + _dot_nt((q * decay_in).astype(BF16), state_t.astype(BF16))
    new_state_t = state_t * jnp.exp(total) + _dot(v.T.astype(BF16), (k * decay_out).astype(BF16))
    return out, new_state_t


def _hgrn_kernel(*refs, seq_len, has_state, want_state):
    q_ref, ff_ref, fb_ref, v_ref, g_ref, lb_ref, nw_ref, scan_ref, masks_ref = refs[:9]
    pos = 9
    s0_ref = None
    if has_state:
        s0_ref = refs[pos]
        pos += 1
    o_ref = refs[pos]
    pos += 1
    sfin_ref = None
    if want_state:
        sfin_ref = refs[pos]
        pos += 1
    accf_ref, accb_ref, bf_scr, bb_scr = refs[pos:pos + 4]

    n_blocks = seq_len // SCAN_BLOCK

    def scan_step(j, carry):
        st_f, st_b = carry
        rows_f = _block_rows(j)
        rows_b = _block_rows(n_blocks - 1 - j)
        out_f, st_f = _hgrn_block(0, rows_f, st_f, q_ref, ff_ref, v_ref, lb_ref[0], scan_ref,
                                  masks_ref, bf_scr)
        out_b, st_b = _hgrn_block(1, rows_b, st_b, q_ref, fb_ref, v_ref, lb_ref[1], scan_ref,
                                  masks_ref, bb_scr)
        accf_ref[rows_f, :] = out_f
        accb_ref[rows_b, :] = out_b
        return st_f, st_b

    if has_state:
        init = (s0_ref[0].T, s0_ref[1].T)
    else:
        init = (jnp.zeros((D_HEAD, D_HEAD), F32),) * 2
    fin_f, fin_b = lax.fori_loop(0, n_blocks, scan_step, init)
    if want_state:
        sfin_ref[0] = fin_f.T
        sfin_ref[1] = fin_b.T

    norm_w = nw_ref[...]

    def finish(j, _):
        rows = _block_rows(j)
        o = accf_ref[rows, :] + accb_ref[rows, :]
        o = o * lax.rsqrt(jnp.mean(o * o, axis=-1, keepdims=True) + EPS) * norm_w
        o_ref[rows, :] = (o * _silu(g_ref[rows, :])).astype(o_ref.dtype)
        return 0

    lax.fori_loop(0, n_blocks, finish, 0)


def _hgrn_mixer(proj, lb_l, norm_w, state0, batch, seq_len, want_state, tables):
    scan, masks = tables
    has_state = state0 is not None
    n_tok = batch * seq_len

    def col(section):
        return pl.BlockSpec((seq_len, D_HEAD), lambda b, h: (b, section * N_HEADS + h))

    in_specs = [col(0), col(1), col(2), col(3), col(4),
                pl.BlockSpec((2, None, 1, D_HEAD), lambda b, h: (0, h, 0, 0)),
                pl.BlockSpec((None, 1, D_HEAD), lambda b, h: (h, 0, 0)),
                pl.BlockSpec(scan.shape, lambda b, h: (0, 0, 0)),
                pl.BlockSpec(masks.shape, lambda b, h: (0, 0, 0, 0))]
    args = [proj, proj, proj, proj, proj, lb_l, norm_w, scan, masks]
    state_spec = pl.BlockSpec((None, 2, None, D_HEAD, D_HEAD), lambda b, h: (b, 0, h, 0, 0))
    if has_state:
        in_specs.append(state_spec)
        args.append(state0)
    out_shape = [jax.ShapeDtypeStruct((n_tok, GROUP_WIDTH), BF16)]
    out_specs = [pl.BlockSpec((seq_len, D_HEAD), lambda b, h: (b, h))]
    if want_state:
        out_shape.append(jax.ShapeDtypeStruct((batch, 2, N_HEADS, D_HEAD, D_HEAD), F32))
        out_specs.append(state_spec)
    res = pl.pallas_call(
        functools.partial(_hgrn_kernel, seq_len=seq_len, has_state=has_state, want_state=want_state),
        grid=(batch, N_HEADS),
        in_specs=in_specs,
        out_specs=out_specs,
        out_shape=out_shape,
        scratch_shapes=[pltpu.VMEM((seq_len, D_HEAD), F32), pltpu.VMEM((seq_len, D_HEAD), F32),
                        pltpu.VMEM((SCAN_BLOCK, D_HEAD), F32), pltpu.VMEM((SCAN_BLOCK, D_HEAD), F32)],
        compiler_params=_params("arbitrary", "arbitrary"),
        name="hgrn2_mixer",
    )(*args)
    return res if want_state else (res[0], None)


def _dwconv(x, taps, bias, grid_w):
    n = x.shape[0]
    col = lax.broadcasted_iota(jnp.int32, (n, 1), 0) % grid_w
    left = jnp.where(col == 0, 0.0, pltpu.roll(x, 1, 0))
    right = jnp.where(col == grid_w - 1, 0.0, pltpu.roll(x, n - 1, 0))

    def kernel_row(i):
        return left * taps[3 * i] + x * taps[3 * i + 1] + right * taps[3 * i + 2]

    y = kernel_row(1) + bias
    if grid_w < n:
        pad = jnp.zeros((grid_w, x.shape[1]), x.dtype)
        y = y + jnp.concatenate([pad, kernel_row(0)[:n - grid_w]], axis=0)
        y = y + jnp.concatenate([kernel_row(2)[grid_w:], pad], axis=0)
    return y


def _mlstm_block(d, rows, state, m_run, qc_ref, kc_ref, v_ref, gates_ref, gate_bias, tri_ref,
                 strict_ref, eye_ref):
    c = SCAN_BLOCK
    q = qc_ref[rows, :]
    k = kc_ref[rows, :]
    v = v_ref[rows, :]
    gates = gates_ref[rows, :] + gate_bias
    log_i = gates[:, d:d + 1]
    fz = gates[:, 2 + d:3 + d]
    log_f = _log_sigmoid(fz)

    tri = tri_ref[d]
    vis = (strict_ref[d] + eye_ref[...]) > 0.5
    mix = log_f * strict_ref[d] + log_i * eye_ref[...]
    d2 = _dot(tri, _split_bf16(mix))
    d_log = d2[:, :c] + d2[:, c:]
    b2 = _dot(tri, _split_bf16(jnp.broadcast_to(log_f, (c, D_HEAD))))
    b_run = b2[:, :D_HEAD] + b2[:, D_HEAD:]

    inter_log = b_run + m_run
    d_max = jnp.max(jnp.where(vis, d_log, -jnp.inf), axis=1, keepdims=True)
    m_t = jnp.maximum(inter_log[:, 0:1], d_max)
    p = jnp.where(vis, jnp.exp(d_log - m_t), 0.0)
    scores = _dot_nt(q.astype(BF16), k.astype(BF16)) * p
    lane = lax.broadcasted_iota(jnp.int32, (c, D_HEAD), 1)
    ones_col = jnp.where(lane == 0, 1.0, 0.0)
    v_aug = jnp.concatenate([v, ones_col], axis=1).astype(BF16)
    intra = _dot(scores.astype(BF16), v_aug)
    inter = _dot(q.astype(BF16), state.astype(BF16))
    a_inter = jnp.exp(inter_log[:, 0:1] - m_t)
    num = intra[:, :D_HEAD] + a_inter * inter[:, :D_HEAD]
    den = intra[:, D_HEAD:D_HEAD + 1] + a_inter * inter[:, D_HEAD:D_HEAD + 1]
    out = num / jnp.maximum(jnp.abs(den), jnp.exp(-m_t))

    last_row = c - 1 if d == 0 else 0
    b_last = b_run[last_row:last_row + 1]
    w_end = b_last - b_run + log_i
    m_loc = jnp.max(w_end, axis=0, keepdims=True)
    p_end = jnp.exp(w_end - m_loc)
    kv = _dot((k * p_end).T.astype(BF16), v_aug)
    m_new = jnp.maximum(b_last + m_run, m_loc)
    a = jnp.exp(b_last + m_run - m_new)
    g = jnp.exp(m_loc - m_new)
    new_state = a[:, 0:1] * state + g[:, 0:1] * kv
    return out, new_state, m_new


def _mlstm_kernel(*refs, seq_len, grid_w, has_state, want_state):
    (q_ref, k_ref, v_ref, og_ref, gates_ref, qtap_ref, ktap_ref, qb_ref, kb_ref, gb_ref, nw_ref,
     tri_ref, strict_ref, eye_ref) = refs[:14]
    pos = 14
    c0_ref = n0_ref = m0_ref = None
    if has_state:
        c0_ref, n0_ref, m0_ref = refs[pos:pos + 3]
        pos += 3
    o_ref = refs[pos]
    pos += 1
    cfin_ref = nfin_ref = mfin_ref = None
    if want_state:
        cfin_ref, nfin_ref, mfin_ref = refs[pos:pos + 3]
        pos += 3
    qc_ref, kc_ref, accf_ref, accb_ref = refs[pos:pos + 4]

    n_blocks = seq_len // SCAN_BLOCK
    qc_ref[...] = _silu(_dwconv(q_ref[...], qtap_ref[...], qb_ref[...], grid_w))
    kc_ref[...] = _silu(_dwconv(k_ref[...], ktap_ref[...], kb_ref[...], grid_w)) * (D_HEAD ** -0.5)

    gate_bias = gb_ref[...]

    def scan_step(j, carry):
        st_f, m_f, st_b, m_b = carry
        rows_f = _block_rows(j)
        rows_b = _block_rows(n_blocks - 1 - j)
        out_f, st_f, m_f = _mlstm_block(0, rows_f, st_f, m_f, qc_ref, kc_ref, v_ref, gates_ref,
                                        gate_bias, tri_ref, strict_ref, eye_ref)
        out_b, st_b, m_b = _mlstm_block(1, rows_b, st_b, m_b, qc_ref, kc_ref, v_ref, gates_ref,
                                        gate_bias, tri_ref, strict_ref, eye_ref)
        accf_ref[rows_f, :] = out_f
        accb_ref[rows_b, :] = out_b
        return st_f, m_f, st_b, m_b

    init = []
    for d in range(2):
        if has_state:
            lane = lax.broadcasted_iota(jnp.int32, (1, D_HEAD), 1)
            n_col = jnp.where(lane == 0, n0_ref[d], 0.0)
            init += [jnp.concatenate([c0_ref[d], n_col], axis=1), m0_ref[d]]
        else:
            init += [jnp.zeros((D_HEAD, 2 * D_HEAD), F32), jnp.zeros((1, D_HEAD), F32)]
    fin = lax.fori_loop(0, n_blocks, scan_step, tuple(init))
    if want_state:
        for d in range(2):
            cfin_ref[d] = fin[2 * d][:, :D_HEAD]
            nfin_ref[d] = fin[2 * d][:, D_HEAD:D_HEAD + 1]
            mfin_ref[d] = fin[2 * d + 1]

    norm_w = nw_ref[...]

    def finish(j, _):
        rows = _block_rows(j)
        o = accf_ref[rows, :] + accb_ref[rows, :]
        o = o * lax.rsqrt(jnp.mean(o * o, axis=-1, keepdims=True) + EPS) * norm_w
        o_ref[rows, :] = (o * _sigmoid(og_ref[rows, :])).astype(o_ref.dtype)
        return 0

    lax.fori_loop(0, n_blocks, finish, 0)


def _mlstm_mixer(proj, conv_taps, conv_bias, gate_bias, norm_w, states0, batch, seq_len, grid_w,
                 want_state, tables):
    tri, strict, eye = tables
    has_state = states0 is not None
    n_tok = batch * seq_len

    def col(block_index):
        return pl.BlockSpec((seq_len, D_HEAD), lambda b, h: (b, block_index(h)))

    def const(arr):
        return pl.BlockSpec(arr.shape, lambda b, h: (0,) * arr.ndim)

    def per_head(offset):
        return pl.BlockSpec((None, 1, D_HEAD), lambda b, h: (offset + h, 0, 0))

    in_specs = [col(lambda h: 5 * N_HEADS + h), col(lambda h: 6 * N_HEADS + h),
                col(lambda h: 7 * N_HEADS + h), col(lambda h: 8 * N_HEADS + h),
                col(lambda h: 9 * N_HEADS + h),
                pl.BlockSpec((9, None, 1, D_HEAD), lambda b, h: (0, h, 0, 0)),
                pl.BlockSpec((9, None, 1, D_HEAD), lambda b, h: (0, N_HEADS + h, 0, 0)),
                per_head(0), per_head(N_HEADS), per_head(0), per_head(0),
                const(tri), const(strict), const(eye)]
    args = [proj, proj, proj, proj, proj, conv_taps, conv_taps, conv_bias, conv_bias, gate_bias,
            norm_w, tri, strict, eye]

    def state_spec(rows, lanes):
        return pl.BlockSpec((None, 2, None, rows, lanes), lambda b, h: (b, 0, h, 0, 0))

    state_specs = [state_spec(D_HEAD, D_HEAD), state_spec(D_HEAD, 1), state_spec(1, D_HEAD)]
    if has_state:
        in_specs += state_specs
        args += list(states0)
    out_shape = [jax.ShapeDtypeStruct((n_tok, GROUP_WIDTH), BF16)]
    out_specs = [pl.BlockSpec((seq_len, D_HEAD), lambda b, h: (b, h))]
    if want_state:
        out_shape += [jax.ShapeDtypeStruct((batch, 2, N_HEADS, D_HEAD, D_HEAD), F32),
                      jax.ShapeDtypeStruct((batch, 2, N_HEADS, D_HEAD, 1), F32),
                      jax.ShapeDtypeStruct((batch, 2, N_HEADS, 1, D_HEAD), F32)]
        out_specs += state_specs
    res = pl.pallas_call(
        functools.partial(_mlstm_kernel, seq_len=seq_len, grid_w=grid_w, has_state=has_state,
                          want_state=want_state),
        grid=(batch, N_HEADS),
        in_specs=in_specs,
        out_specs=out_specs,
        out_shape=out_shape,
        scratch_shapes=[pltpu.VMEM((seq_len, D_HEAD), F32)] * 4,
        compiler_params=_params("arbitrary", "arbitrary"),
        name="mlstm_mixer",
    )(*args)
    return (res[0], tuple(res[1:])) if want_state else (res[0], None)


def _post_kernel(x_ref, mh_ref, mm_ref, g1_ref, sc2_ref, sh2_ref, g2_ref, n2_ref, woh_ref, wom_ref,
                 wg_ref, wu_ref, wd_ref, fn_ref, o_ref, x1_ref, h2_ref, acc_ref, *, final_norm):
    j = pl.program_id(1)

    @pl.when(j == 0)
    def _():
        mixed = _dot(mh_ref[...], woh_ref[...]) + _dot(mm_ref[...], wom_ref[...])
        x1 = x_ref[...] + g1_ref[...] * mixed
        x1_ref[...] = x1
        h2 = _rmsnorm(x1, n2_ref[...]) * (1.0 + sc2_ref[...]) + sh2_ref[...]
        h2_ref[...] = h2.astype(BF16)
        acc_ref[...] = jnp.zeros_like(acc_ref)

    hb = h2_ref[...]
    act = _silu(_dot(hb, wg_ref[...])) * _dot(hb, wu_ref[...])
    acc_ref[...] += _dot(act.astype(BF16), wd_ref[...])

    @pl.when(j == pl.num_programs(1) - 1)
    def _():
        x2 = x1_ref[...] + g2_ref[...] * acc_ref[...]
        if final_norm:
            x2 = _rmsnorm(x2, fn_ref[...])
        o_ref[...] = x2


def _post_mixer(x, mix_h, mix_m, mod_l, norm2_w, w_out_h, w_out_m, w_gate, w_up, w_down,
                final_norm_w, row_of_tile, tm, final_norm):
    n_tok = x.shape[0]
    tf = D_FF // 2
    row = lambda i, j: (i, 0)
    fixed = lambda i, j: (0, 0)
    return pl.pallas_call(
        functools.partial(_post_kernel, final_norm=final_norm),
        grid=(n_tok // tm, D_FF // tf),
        in_specs=[
            pl.BlockSpec((tm, D_MODEL), row),
            pl.BlockSpec((tm, GROUP_WIDTH), row),
            pl.BlockSpec((tm, GROUP_WIDTH), row),
            _mod_spec(2, row_of_tile), _mod_spec(4, row_of_tile), _mod_spec(3, row_of_tile),
            _mod_spec(5, row_of_tile),
            pl.BlockSpec((1, D_MODEL), fixed),
            pl.BlockSpec((GROUP_WIDTH, D_MODEL), fixed),
            pl.BlockSpec((GROUP_WIDTH, D_MODEL), fixed),
            pl.BlockSpec((D_MODEL, tf), lambda i, j: (0, j)),
            pl.BlockSpec((D_MODEL, tf), lambda i, j: (0, j)),
            pl.BlockSpec((tf, D_MODEL), lambda i, j: (j, 0)),
            pl.BlockSpec((1, D_MODEL), fixed),
        ],
        out_specs=pl.BlockSpec((tm, D_MODEL), row),
        out_shape=jax.ShapeDtypeStruct((n_tok, D_MODEL), F32),
        scratch_shapes=[pltpu.VMEM((tm, D_MODEL), F32), pltpu.VMEM((tm, D_MODEL), BF16),
                        pltpu.VMEM((tm, D_MODEL), F32)],
        compiler_params=_params("arbitrary", "arbitrary"),
        name="out_projection_ffn",
    )(x, mix_h, mix_m, mod_l, mod_l, mod_l, mod_l, norm2_w, w_out_h, w_out_m, w_gate, w_up, w_down,
      final_norm_w)


def _pack_in_projection(w_in_l):
    main = w_in_l[:, :N_MAIN_COLS]
    gates = w_in_l[:, N_MAIN_COLS:].reshape(D_MODEL, 4, N_HEADS).transpose(0, 2, 1)
    gates = jnp.pad(gates, ((0, 0), (0, 0), (0, D_HEAD - 4))).reshape(D_MODEL, GROUP_WIDTH)
    return jnp.concatenate([main, gates], axis=1).astype(BF16)


def kernel(x_prompt, x_sample, state_hgrn, state_mlstm_c, state_mlstm_n, state_mlstm_m, c, c_ctx,
           norm1_w, norm2_w, w_mod, b_mod, w_in, conv_w, conv_b, ml_gate_b, hg_lb_logits,
           hg_norm_w, ml_norm_w, w_out, w_gate, w_up, w_down, final_norm_w):
    depth = w_in.shape[0]
    n_ctx, ctx_len, _ = x_prompt.shape
    n_lat, lat_len, _ = x_sample.shape
    assert 1 + n_lat <= N_MOD_ROWS and ctx_len % SCAN_BLOCK == 0 and lat_len % SCAN_BLOCK == 0

    hg_tables = _hgrn_tables()
    ml_tables = _mlstm_tables()
    lb_all = _lower_bounds(hg_lb_logits)
    cond = jnp.zeros((N_MOD_ROWS, D_MODEL), F32).at[0].set(c_ctx).at[1:1 + n_lat].set(c)
    mod = _modulation(cond, w_mod, b_mod).reshape(depth, N_MOD_ROWS, 6, 1, D_MODEL)

    tm_in, tm_post = 256, 512
    ctx_row_in = ctx_row_post = lambda i: 0
    lat_row_in = lambda i: 1 + i // (lat_len // tm_in)
    lat_row_post = lambda i: 1 + i // (lat_len // tm_post)

    xp = x_prompt.reshape(n_ctx * ctx_len, D_MODEL)
    xs = x_sample.reshape(n_lat * lat_len, D_MODEL)
    fin_w = final_norm_w.reshape(1, D_MODEL)
    hg_fin, mc_fin, mn_fin, mm_fin = [], [], [], []
    for l in range(depth):
        w_in_l = _pack_in_projection(w_in[l])
        n1 = norm1_w[l].reshape(1, D_MODEL)
        n2 = norm2_w[l].reshape(1, D_MODEL)
        w_out_h = w_out[l, :GROUP_WIDTH].astype(BF16)
        w_out_m = w_out[l, GROUP_WIDTH:].astype(BF16)
        wg, wu, wd = w_gate[l].astype(BF16), w_up[l].astype(BF16), w_down[l].astype(BF16)
        taps = conv_w[l].reshape(9, 2 * N_HEADS, 1, D_HEAD)
        cbias = conv_b[l].reshape(2 * N_HEADS, 1, D_HEAD)
        gbias = jnp.pad(ml_gate_b[l].reshape(4, N_HEADS).T, ((0, 0), (0, D_HEAD - 4)))
        gbias = gbias.reshape(N_HEADS, 1, D_HEAD)
        hg_nw = hg_norm_w[l].reshape(N_HEADS, 1, D_HEAD)
        ml_nw = ml_norm_w[l].reshape(N_HEADS, 1, D_HEAD)
        last = l == depth - 1

        proj = _in_projection(xp, mod[l], n1, w_in_l, ctx_row_in, tm_in)
        mix_h, s_h = _hgrn_mixer(proj, lb_all[l], hg_nw, None, n_ctx, ctx_len, True, hg_tables)
        mix_m, s_m = _mlstm_mixer(proj, taps, cbias, gbias, ml_nw, None, n_ctx, ctx_len, ctx_len,
                                  True, ml_tables)
        xp = _post_mixer(xp, mix_h, mix_m, mod[l], n2, w_out_h, w_out_m, wg, wu, wd, fin_w,
                         ctx_row_post, tm_post, last)
        hg_fin.append(s_h)
        mc_fin.append(s_m[0])
        mn_fin.append(s_m[1][..., 0])
        mm_fin.append(s_m[2][..., 0, 0])

        cached = (state_mlstm_c[:, l].astype(F32), state_mlstm_n[:, l].astype(F32)[..., None],
                  jnp.broadcast_to(state_mlstm_m[:, l].astype(F32)[..., None, None],
                                   (n_lat, 2, N_HEADS, 1, D_HEAD)))
        proj = _in_projection(xs, mod[l], n1, w_in_l, lat_row_in, tm_in)
        mix_h, _ = _hgrn_mixer(proj, lb_all[l], hg_nw, state_hgrn[:, l].astype(F32), n_lat, lat_len,
                               False, hg_tables)
        mix_m, _ = _mlstm_mixer(proj, taps, cbias, gbias, ml_nw, cached, n_lat, lat_len,
                                LATENT_GRID_W, False, ml_tables)
        xs = _post_mixer(xs, mix_h, mix_m, mod[l], n2, w_out_h, w_out_m, wg, wu, wd, fin_w,
                         lat_row_post, tm_post, last)

    return (xp.reshape(x_prompt.shape), xs.reshape(x_sample.shape), jnp.stack(hg_fin, axis=1),
            jnp.stack(mc_fin, axis=1), jnp.stack(mn_fin, axis=1), jnp.stack(mm_fin, axis=1))
```

```python
import functools

import numpy as np
import jax
import jax.numpy as jnp
from jax import lax
from jax.experimental import pallas as pl
from jax.experimental.pallas import tpu as pltpu

F32 = jnp.float32
BF16 = jnp.bfloat16

D_MODEL = 1024
N_HEADS = 4
D_HEAD = 128
GROUP_WIDTH = N_HEADS * D_HEAD
D_FF = 2816
LATENT_GRID_W = 64
EPS = 1e-6
N_MOD_ROWS = 16
N_MAIN_COLS = 9 * GROUP_WIDTH
N_GATE_COLS = 4 * N_HEADS
N_PROJ_COLS = N_MAIN_COLS + D_HEAD
SCAN_BLOCK = 128
HGRN_BLOCK = 256
LEVEL_HALVES = tuple(HGRN_BLOCK >> (i + 1) for i in range(HGRN_BLOCK.bit_length() - 1))
LOG2_E = 1.4426950408889634

VMEM_LIMIT_BYTES = 56 * 1024 * 1024


def _params(*semantics):
    return pltpu.CompilerParams(dimension_semantics=semantics, vmem_limit_bytes=VMEM_LIMIT_BYTES)


def _sigmoid(x):
    return 1.0 / (1.0 + jnp.exp(-x))


def _silu(x):
    return x * _sigmoid(x)


def _log_sigmoid(x):
    return jnp.minimum(x, 0.0) - jnp.log(1.0 + jnp.exp(-jnp.abs(x)))


def _dot(a, b):
    return jnp.dot(a, b, preferred_element_type=F32)


def _dot_nt(a, b):
    return lax.dot_general(a, b, (((1,), (1,)), ((), ())), preferred_element_type=F32)


def _split_bf16(x):
    hi = x.astype(BF16)
    lo = (x - hi.astype(F32)).astype(BF16)
    return jnp.concatenate([hi, lo], axis=1)


def _block_rows(blk, size=SCAN_BLOCK):
    return pl.ds(pl.multiple_of(blk * size, size), size)


def _hgrn_tables():
    c = HGRN_BLOCK
    t = np.arange(c)[:, None]
    u = np.arange(c)[None, :]
    scan = np.stack([u <= t, u >= t]).astype(np.float32)
    masks = np.zeros((2, 1 + len(LEVEL_HALVES), c, c), np.float32)
    masks[:, 0] = t == u
    for i, m in enumerate(LEVEL_HALVES):
        upper_t = (t % (2 * m)) >= m
        upper_u = (u % (2 * m)) >= m
        same = (t // (2 * m)) == (u // (2 * m))
        masks[0, 1 + i] = same & upper_t & ~upper_u
        masks[1, 1 + i] = same & ~upper_t & upper_u
    return jnp.asarray(scan, BF16), jnp.asarray(masks, BF16)


def _mlstm_tables():
    c = SCAN_BLOCK
    t = np.arange(c)[:, None]
    u = np.arange(c)[None, :]
    tri = np.stack([u <= t, u >= t]).astype(np.float32)
    strict = np.stack([t > u, t < u]).astype(np.float32)
    sel = np.zeros((N_HEADS, 2, 2 * D_HEAD, 2 * D_HEAD), np.float32)
    for h in range(N_HEADS):
        for d in range(2):
            for part in range(2):
                sel[h, d, part * D_HEAD + d * N_HEADS + h, :D_HEAD] = 1.0
                sel[h, d, part * D_HEAD + (2 + d) * N_HEADS + h, D_HEAD:] = 1.0
    return (jnp.asarray(tri, BF16), jnp.asarray(strict, F32),
            jnp.asarray(np.eye(c, dtype=np.float32)), jnp.asarray(sel, BF16))


def _lower_bound_kernel(logit_ref, lb_ref):
    z = logit_ref[...]
    z = z - jnp.max(z, axis=0, keepdims=True)
    e = jnp.exp(z)
    p = e / jnp.sum(e, axis=0, keepdims=True)
    depth = z.shape[0]
    run = p[0:1]
    first = run
    for l in range(depth):
        if l > 0:
            run = run + p[l:l + 1]
        lb_ref[l:l + 1, :] = run - first


def _lower_bounds(hg_lb_logits):
    depth = hg_lb_logits.shape[0]
    flat = hg_lb_logits.astype(F32).reshape(depth, 2 * GROUP_WIDTH)
    lb = pl.pallas_call(
        _lower_bound_kernel,
        out_shape=jax.ShapeDtypeStruct(flat.shape, F32),
        name="hgrn_lower_bounds",
    )(flat)
    return lb.reshape(depth, 2, N_HEADS, 1, D_HEAD)


def _mod_kernel(cond_ref, w_ref, b_ref, o_ref):
    c = cond_ref[...]
    o_ref[0] = _dot(_silu(c).astype(BF16), w_ref[0].astype(BF16)) + b_ref[0]


def _modulation(cond, w_mod, b_mod):
    depth = w_mod.shape[0]
    n_out = w_mod.shape[2]
    tn = 1536
    return pl.pallas_call(
        _mod_kernel,
        grid=(depth, n_out // tn),
        in_specs=[
            pl.BlockSpec((N_MOD_ROWS, D_MODEL), lambda l, j: (0, 0)),
            pl.BlockSpec((1, D_MODEL, tn), lambda l, j: (l, 0, j)),
            pl.BlockSpec((1, 1, tn), lambda l, j: (l, 0, j)),
        ],
        out_specs=pl.BlockSpec((1, N_MOD_ROWS, tn), lambda l, j: (l, 0, j)),
        out_shape=jax.ShapeDtypeStruct((depth, N_MOD_ROWS, n_out), F32),
        compiler_params=_params("arbitrary", "arbitrary"),
        name="adaln_modulation",
    )(cond, w_mod, b_mod.reshape(depth, 1, n_out))


def _rmsnorm(x, w):
    return x * lax.rsqrt(jnp.mean(x * x, axis=-1, keepdims=True) + EPS) * w


def _inproj_kernel(x_ref, sc_ref, sh_ref, nw_ref, w_ref, o_ref):
    h = _rmsnorm(x_ref[...], nw_ref[...]) * (1.0 + sc_ref[...]) + sh_ref[...]
    hb = h.astype(BF16)
    for n0 in range(0, N_PROJ_COLS, GROUP_WIDTH):
        n1 = min(n0 + GROUP_WIDTH, N_PROJ_COLS)
        o_ref[:, n0:n1] = _dot(hb, w_ref[:, n0:n1])


def _mod_spec(which, row_of_tile):
    return pl.BlockSpec((None, None, 1, D_MODEL), lambda i, *_: (row_of_tile(i), which, 0, 0))


def _in_projection(x, mod_l, norm_w, w_in, layer, row_of_tile, tm):
    n_tok = x.shape[0]
    return pl.pallas_call(
        _inproj_kernel,
        grid=(n_tok // tm,),
        in_specs=[
            pl.BlockSpec((tm, D_MODEL), lambda i: (i, 0)),
            _mod_spec(1, row_of_tile),
            _mod_spec(0, row_of_tile),
            pl.BlockSpec((1, D_MODEL), lambda i: (0, 0)),
            pl.BlockSpec((None, D_MODEL, N_PROJ_COLS), lambda i: (layer, 0, 0)),
        ],
        out_specs=pl.BlockSpec((tm, N_PROJ_COLS), lambda i: (i, 0)),
        out_shape=jax.ShapeDtypeStruct((n_tok, N_PROJ_COLS), F32),
        compiler_params=_params("arbitrary"),
        name="norm_in_projection",
    )(x, mod_l, mod_l, norm_w, w_in)


def _hgrn_block(d, q, v, fz, lb, scan_ref, masks_ref, b_scr):
    c = HGRN_BLOCK
    e = jnp.exp(-jnp.abs(fz))
    inv = 1.0 / (1.0 + e)
    pos = fz >= 0.0
    k = (1.0 - lb) * (jnp.where(pos, e, 1.0) * inv)
    f = lb + (1.0 - lb) * (jnp.where(pos, 1.0, e) * inv)
    logf = jnp.where(lb > 0.0, jnp.log(f), jnp.minimum(fz, 0.0) - jnp.log(1.0 + e))

    b2 = _dot(scan_ref[d], _split_bf16(logf * LOG2_E))
    b = b2[:, :D_HEAD] + b2[:, D_HEAD:]
    total_row = c - 1 if d == 0 else 0
    total = b[total_row:total_row + 1]
    decay_in = jnp.exp2(b)
    decay_out = jnp.exp2(total - b)
    vb = v.astype(BF16)

    b_scr[...] = b

    scores = _dot_nt(q.astype(BF16), k.astype(BF16)).astype(BF16) * masks_ref[d, 0]
    t_idx = lax.broadcasted_iota(jnp.int32, (c, D_HEAD), 0)
    f_prev = pltpu.roll(f, 1, 0)
    f_next = pltpu.roll(f, c - 1, 0)
    for i, m in enumerate(LEVEL_HALVES):
        if m >= 8:
            parts = []
            for g in range(c // (2 * m)):
                ref_row = b_scr[g * 2 * m + (m - 1 if d == 0 else m):
                                g * 2 * m + (m - 1 if d == 0 else m) + 1, :]
                low = b[g * 2 * m:g * 2 * m + m]
                high = b[g * 2 * m + m:(g + 1) * 2 * m]
                if d == 0:
                    parts += [jnp.exp2(ref_row - low), jnp.exp2(high - ref_row)]
                else:
                    parts += [jnp.exp2(low - ref_row), jnp.exp2(ref_row - high)]
            e_lvl = jnp.concatenate(parts, axis=0)
        elif m == 4:
            parts = []
            for g in range(c // (2 * m)):
                r = g * 2 * m + (m - 1 if d == 0 else m)
                parts.append(jnp.exp2(-jnp.abs(b[g * 2 * m:(g + 1) * 2 * m] - b_scr[r:r + 1, :])))
            e_lvl = jnp.concatenate(parts, axis=0)
        elif m == 2:
            ph = t_idx % 4
            if d == 0:
                e_lvl = jnp.where(ph == 0, f_next, jnp.where(ph == 1, 1.0,
                                  jnp.where(ph == 2, f, f * f_prev)))
            else:
                e_lvl = jnp.where(ph == 0, f * f_next, jnp.where(ph == 1, f,
                                  jnp.where(ph == 2, 1.0, f_prev)))
        else:
            e_lvl = jnp.where(t_idx % 2 == (1 - d), f, 1.0)
        s_lvl = _dot_nt((q * e_lvl).astype(BF16), (k * e_lvl).astype(BF16))
        scores = scores + s_lvl.astype(BF16) * masks_ref[d, 1 + i]

    intra = _dot(scores, vb)
    q_in = (q * decay_in).astype(BF16)
    kv_t = _dot(v.T.astype(BF16), (k * decay_out).astype(BF16))
    return intra, q_in, kv_t, jnp.exp2(total)


def _hgrn_kernel(*refs, seq_len, has_state, want_state):
    q_ref, ff_ref, fb_ref, v_ref, g_ref, lb_ref, nw_ref, scan_ref, masks_ref = refs[:9]
    pos = 9
    s0_ref = None
    if has_state:
        s0_ref = refs[pos]
        pos += 1
    o_ref = refs[pos]
    pos += 1
    sfin_ref = None
    if want_state:
        sfin_ref = refs[pos]
        pos += 1
    acc_ref, b_scr = refs[pos:pos + 2]

    n_blocks = seq_len // HGRN_BLOCK
    fz_refs = (ff_ref, fb_ref)

    def scan_step(j, carry):
        states = list(carry)
        for d in range(2):
            rows = _block_rows(j if d == 0 else n_blocks - 1 - j, HGRN_BLOCK)
            intra, q_in, kv_t, decay = _hgrn_block(d, _silu(q_ref[rows, :]), v_ref[rows, :],
                                                   fz_refs[d][rows, :], lb_ref[d], scan_ref,
                                                   masks_ref, b_scr.at[d])
            acc_ref[d, rows, :] = intra + _dot_nt(q_in, states[d].astype(BF16))
            states[d] = states[d] * decay + kv_t
        return tuple(states)

    if has_state:
        init = (s0_ref[0].T, s0_ref[1].T)
    else:
        init = (jnp.zeros((D_HEAD, D_HEAD), F32),) * 2
    fin_f, fin_b = lax.fori_loop(0, n_blocks, scan_step, init, unroll=min(2, n_blocks))
    if want_state:
        sfin_ref[0] = fin_f.T
        sfin_ref[1] = fin_b.T

    norm_w = nw_ref[...]

    def finish(j, _):
        rows = _block_rows(j, HGRN_BLOCK)
        o = acc_ref[0, rows, :] + acc_ref[1, rows, :]
        o = o * lax.rsqrt(jnp.mean(o * o, axis=-1, keepdims=True) + EPS) * norm_w
        o_ref[rows, :] = (o * _silu(g_ref[rows, :])).astype(o_ref.dtype)
        return 0

    lax.fori_loop(0, n_blocks, finish, 0)


def _hgrn_mixer(proj, lb_l, norm_w, state0, batch, seq_len, want_state, tables):
    scan, masks = tables
    has_state = state0 is not None
    n_tok = batch * seq_len

    def col(section):
        return pl.BlockSpec((seq_len, D_HEAD), lambda b, h: (b, section * N_HEADS + h))

    in_specs = [col(0), col(1), col(2), col(3), col(4),
                pl.BlockSpec((2, None, 1, D_HEAD), lambda b, h: (0, h, 0, 0)),
                pl.BlockSpec((None, 1, D_HEAD), lambda b, h: (h, 0, 0)),
                pl.BlockSpec(scan.shape, lambda b, h: (0, 0, 0)),
                pl.BlockSpec(masks.shape, lambda b, h: (0, 0, 0, 0))]
    args = [proj, proj, proj, proj, proj, lb_l, norm_w, scan, masks]
    state_spec = pl.BlockSpec((None, 2, None, D_HEAD, D_HEAD), lambda b, h: (b, 0, h, 0, 0))
    if has_state:
        in_specs.append(state_spec)
        args.append(state0)
    out_shape = [jax.ShapeDtypeStruct((n_tok, GROUP_WIDTH), BF16)]
    out_specs = [pl.BlockSpec((seq_len, D_HEAD), lambda b, h: (b, h))]
    if want_state:
        out_shape.append(jax.ShapeDtypeStruct((batch, 2, N_HEADS, D_HEAD, D_HEAD), F32))
        out_specs.append(state_spec)
    res = pl.pallas_call(
        functools.partial(_hgrn_kernel, seq_len=seq_len, has_state=has_state, want_state=want_state),
        grid=(batch, N_HEADS),
        in_specs=in_specs,
        out_specs=out_specs,
        out_shape=out_shape,
        scratch_shapes=[pltpu.VMEM((2, seq_len, D_HEAD), F32),
                        pltpu.VMEM((2, HGRN_BLOCK, D_HEAD), F32)],
        compiler_params=_params("arbitrary", "arbitrary"),
        name="hgrn2_mixer",
    )(*args)
    return res if want_state else (res[0], None)


def _dwconv_rows(x_ref, taps_ref, bias_ref, y_ref, seq_len, grid_w):
    w = grid_w
    n_rows = seq_len // w
    taps = taps_ref[...]
    bias = jnp.broadcast_to(bias_ref[...], (w, x_ref.shape[1]))
    col = lax.broadcasted_iota(jnp.int32, (w, x_ref.shape[1]), 0)
    y_ref[pl.ds(0, w), :] = jnp.zeros_like(bias)
    y_ref[pl.ds(w, w), :] = bias

    def grid_row(r, _):
        base = pl.multiple_of(r * w, w)
        x = x_ref[pl.ds(base, w), :]
        left = jnp.where(col == 0, 0.0, pltpu.roll(x, 1, 0))
        right = jnp.where(col == w - 1, 0.0, pltpu.roll(x, w - 1, 0))

        def kernel_row(i):
            return left * taps[3 * i] + x * taps[3 * i + 1] + right * taps[3 * i + 2]

        y_ref[pl.ds(base + 2 * w, w), :] = kernel_row(0) + bias
        y_ref[pl.ds(base + w, w), :] += kernel_row(1)
        y_ref[pl.ds(base, w), :] += kernel_row(2)
        return 0

    lax.fori_loop(0, n_rows, grid_row, 0)


def _mlstm_block(d, rows, conv_rows, state, m_run, yq_ref, yk_ref, v_ref, gates_ref, gate_bias,
                 tri_ref, strict_ref, eye_ref, sel_ref):
    c = SCAN_BLOCK
    q = _silu(yq_ref[conv_rows, :])
    k = _silu(yk_ref[conv_rows, :]) * (D_HEAD ** -0.5)
    v = v_ref[rows, :]
    spread = _dot(_split_bf16(gates_ref[rows, :] + gate_bias), sel_ref[d])
    log_i = spread[:, :D_HEAD]
    log_f = _log_sigmoid(spread[:, D_HEAD:])

    tri = tri_ref[d]
    vis = (strict_ref[d] + eye_ref[...]) > 0.5
    mix = log_f * strict_ref[d] + log_i * eye_ref[...]
    d2 = _dot(tri, _split_bf16(mix))
    d_log = d2[:, :c] + d2[:, c:]
    b2 = _dot(tri, _split_bf16(log_f))
    b_run = b2[:, :D_HEAD] + b2[:, D_HEAD:]

    inter_log = b_run + m_run
    d_max = jnp.max(jnp.where(vis, d_log, -jnp.inf), axis=1, keepdims=True)
    m_t = jnp.maximum(inter_log, jnp.broadcast_to(d_max, (c, D_HEAD)))
    p = jnp.where(vis, jnp.exp(d_log - m_t), 0.0)
    scores = _dot_nt(q.astype(BF16), k.astype(BF16)) * p
    v_aug = jnp.concatenate([v, jnp.ones((c, D_HEAD), F32)], axis=1).astype(BF16)
    intra = _dot(scores.astype(BF16), v_aug)
    inter = _dot(q.astype(BF16), state.astype(BF16))
    a_inter = jnp.exp(inter_log - m_t)
    num = intra[:, :D_HEAD] + a_inter * inter[:, :D_HEAD]
    den = intra[:, D_HEAD:] + a_inter * inter[:, D_HEAD:]
    out = num / jnp.maximum(jnp.abs(den), jnp.exp(-m_t))

    last_row = c - 1 if d == 0 else 0
    b_last = b_run[last_row:last_row + 1]
    w_end = b_last - b_run + log_i
    m_loc = jnp.max(w_end, axis=0, keepdims=True)
    p_end = jnp.exp(w_end - m_loc)
    kv = _dot((k * p_end).T.astype(BF16), v_aug)
    m_new = jnp.maximum(b_last + m_run, m_loc)
    a = jnp.exp(b_last + m_run - m_new)
    g = jnp.exp(m_loc - m_new)
    new_state = (jnp.concatenate([a, a], axis=1) * state + jnp.concatenate([g, g], axis=1) * kv)
    return out, new_state, m_new


def _mlstm_kernel(*refs, seq_len, grid_w, has_state, want_state):
    (q_ref, k_ref, v_ref, og_ref, gates_ref, qtap_ref, ktap_ref, qb_ref, kb_ref, gb_ref, nw_ref,
     tri_ref, strict_ref, eye_ref, sel_ref) = refs[:15]
    pos = 15
    c0_ref = n0_ref = m0_ref = None
    if has_state:
        c0_ref, n0_ref, m0_ref = refs[pos:pos + 3]
        pos += 3
    o_ref = refs[pos]
    pos += 1
    cfin_ref = nfin_ref = mfin_ref = None
    if want_state:
        cfin_ref, nfin_ref, mfin_ref = refs[pos:pos + 3]
        pos += 3
    yq_ref, yk_ref, accf_ref, accb_ref = refs[pos:pos + 4]

    n_blocks = seq_len // SCAN_BLOCK
    _dwconv_rows(q_ref, qtap_ref, qb_ref, yq_ref, seq_len, grid_w)
    _dwconv_rows(k_ref, ktap_ref, kb_ref, yk_ref, seq_len, grid_w)

    gate_bias = gb_ref[...]
    conv_align = int(np.gcd(grid_w, SCAN_BLOCK))

    def conv_rows(blk):
        return pl.ds(pl.multiple_of(blk * SCAN_BLOCK + grid_w, conv_align), SCAN_BLOCK)

    def scan_step(j, carry):
        st_f, m_f, st_b, m_b = carry
        blk_b = n_blocks - 1 - j
        rows_f = _block_rows(j)
        rows_b = _block_rows(blk_b)
        out_f, st_f, m_f = _mlstm_block(0, rows_f, conv_rows(j), st_f, m_f, yq_ref, yk_ref, v_ref,
                                        gates_ref, gate_bias, tri_ref, strict_ref, eye_ref, sel_ref)
        out_b, st_b, m_b = _mlstm_block(1, rows_b, conv_rows(blk_b), st_b, m_b, yq_ref, yk_ref,
                                        v_ref, gates_ref, gate_bias, tri_ref, strict_ref, eye_ref,
                                        sel_ref)
        accf_ref[rows_f, :] = out_f
        accb_ref[rows_b, :] = out_b
        return st_f, m_f, st_b, m_b

    init = []
    for d in range(2):
        if has_state:
            n_rep = jnp.broadcast_to(n0_ref[d], (D_HEAD, D_HEAD))
            init += [jnp.concatenate([c0_ref[d], n_rep], axis=1), m0_ref[d]]
        else:
            init += [jnp.zeros((D_HEAD, 2 * D_HEAD), F32), jnp.zeros((1, D_HEAD), F32)]
    fin = lax.fori_loop(0, n_blocks, scan_step, tuple(init), unroll=min(8, n_blocks))
    if want_state:
        for d in range(2):
            cfin_ref[d] = fin[2 * d][:, :D_HEAD]
            nfin_ref[d] = fin[2 * d][:, D_HEAD:D_HEAD + 1]
            mfin_ref[d] = fin[2 * d + 1]

    norm_w = nw_ref[...]

    def finish(j, _):
        rows = _block_rows(j)
        o = accf_ref[rows, :] + accb_ref[rows, :]
        o = o * lax.rsqrt(jnp.mean(o * o, axis=-1, keepdims=True) + EPS) * norm_w
        o_ref[rows, :] = (o * _sigmoid(og_ref[rows, :])).astype(o_ref.dtype)
        return 0

    lax.fori_loop(0, n_blocks, finish, 0)


def _mlstm_mixer(proj, conv_taps, conv_bias, gate_bias, norm_w, states0, batch, seq_len, grid_w,
                 want_state, tables):
    tri, strict, eye, sel = tables
    has_state = states0 is not None
    n_tok = batch * seq_len

    def col(block_index):
        return pl.BlockSpec((seq_len, D_HEAD), lambda b, h: (b, block_index(h)))

    def const(arr):
        return pl.BlockSpec(arr.shape, lambda b, h: (0,) * arr.ndim)

    def per_head(offset):
        return pl.BlockSpec((None, 1, D_HEAD), lambda b, h: (offset + h, 0, 0))

    in_specs = [col(lambda h: 5 * N_HEADS + h), col(lambda h: 6 * N_HEADS + h),
                col(lambda h: 7 * N_HEADS + h), col(lambda h: 8 * N_HEADS + h),
                col(lambda h: 9 * N_HEADS),
                pl.BlockSpec((9, None, 1, D_HEAD), lambda b, h: (0, h, 0, 0)),
                pl.BlockSpec((9, None, 1, D_HEAD), lambda b, h: (0, N_HEADS + h, 0, 0)),
                per_head(0), per_head(N_HEADS), const(gate_bias), per_head(0),
                const(tri), const(strict), const(eye),
                pl.BlockSpec((None,) + sel.shape[1:], lambda b, h: (h, 0, 0, 0))]
    args = [proj, proj, proj, proj, proj, conv_taps, conv_taps, conv_bias, conv_bias, gate_bias,
            norm_w, tri, strict, eye, sel]

    def state_spec(rows, lanes):
        return pl.BlockSpec((None, 2, None, rows, lanes), lambda b, h: (b, 0, h, 0, 0))

    state_specs = [state_spec(D_HEAD, D_HEAD), state_spec(D_HEAD, 1), state_spec(1, D_HEAD)]
    if has_state:
        in_specs += state_specs
        args += list(states0)
    out_shape = [jax.ShapeDtypeStruct((n_tok, GROUP_WIDTH), BF16)]
    out_specs = [pl.BlockSpec((seq_len, D_HEAD), lambda b, h: (b, h))]
    if want_state:
        out_shape += [jax.ShapeDtypeStruct((batch, 2, N_HEADS, D_HEAD, D_HEAD), F32),
                      jax.ShapeDtypeStruct((batch, 2, N_HEADS, D_HEAD, 1), F32),
                      jax.ShapeDtypeStruct((batch, 2, N_HEADS, 1, D_HEAD), F32)]
        out_specs += state_specs
    res = pl.pallas_call(
        functools.partial(_mlstm_kernel, seq_len=seq_len, grid_w=grid_w, has_state=has_state,
                          want_state=want_state),
        grid=(batch, N_HEADS),
        in_specs=in_specs,
        out_specs=out_specs,
        out_shape=out_shape,
        scratch_shapes=[pltpu.VMEM((seq_len + 2 * grid_w, D_HEAD), F32)] * 2
        + [pltpu.VMEM((seq_len, D_HEAD), F32)] * 2,
        compiler_params=_params("arbitrary", "arbitrary"),
        name="mlstm_mixer",
    )(*args)
    return (res[0], tuple(res[1:])) if want_state else (res[0], None)


def _post_kernel(x_ref, mh_ref, mm_ref, g1_ref, sc2_ref, sh2_ref, g2_ref, n2_ref, woh_ref, wom_ref,
                 wg_ref, wu_ref, wd_ref, fn_ref, o_ref, x1_ref, h2_ref, acc_ref, *, final_norm):
    j = pl.program_id(1)

    @pl.when(j == 0)
    def _():
        mixed = _dot(mh_ref[...], woh_ref[...]) + _dot(mm_ref[...], wom_ref[...])
        x1 = x_ref[...] + g1_ref[...] * mixed
        x1_ref[...] = x1
        h2 = _rmsnorm(x1, n2_ref[...]) * (1.0 + sc2_ref[...]) + sh2_ref[...]
        h2_ref[...] = h2.astype(BF16)
        acc_ref[...] = jnp.zeros_like(acc_ref)

    hb = h2_ref[...]
    act = _silu(_dot(hb, wg_ref[...])) * _dot(hb, wu_ref[...])
    acc_ref[...] += _dot(act.astype(BF16), wd_ref[...])

    @pl.when(j == pl.num_programs(1) - 1)
    def _():
        x2 = x1_ref[...] + g2_ref[...] * acc_ref[...]
        if final_norm:
            x2 = _rmsnorm(x2, fn_ref[...])
        o_ref[...] = x2


def _post_mixer(x, mix_h, mix_m, mod_l, norm2_w, w_out, w_gate, w_up, w_down, final_norm_w, layer,
                row_of_tile, tm, final_norm):
    n_tok = x.shape[0]
    tf = D_FF // 2
    row = lambda i, j: (i, 0)
    fixed = lambda i, j: (0, 0)
    return pl.pallas_call(
        functools.partial(_post_kernel, final_norm=final_norm),
        grid=(n_tok // tm, D_FF // tf),
        in_specs=[
            pl.BlockSpec((tm, D_MODEL), row),
            pl.BlockSpec((tm, GROUP_WIDTH), row),
            pl.BlockSpec((tm, GROUP_WIDTH), row),
            _mod_spec(2, row_of_tile), _mod_spec(4, row_of_tile), _mod_spec(3, row_of_tile),
            _mod_spec(5, row_of_tile),
            pl.BlockSpec((1, D_MODEL), fixed),
            pl.BlockSpec((None, GROUP_WIDTH, D_MODEL), lambda i, j: (layer, 0, 0)),
            pl.BlockSpec((None, GROUP_WIDTH, D_MODEL), lambda i, j: (layer, 1, 0)),
            pl.BlockSpec((None, D_MODEL, tf), lambda i, j: (layer, 0, j)),
            pl.BlockSpec((None, D_MODEL, tf), lambda i, j: (layer, 0, j)),
            pl.BlockSpec((None, tf, D_MODEL), lambda i, j: (layer, j, 0)),
            pl.BlockSpec((1, D_MODEL), fixed),
        ],
        out_specs=pl.BlockSpec((tm, D_MODEL), row),
        out_shape=jax.ShapeDtypeStruct((n_tok, D_MODEL), F32),
        scratch_shapes=[pltpu.VMEM((tm, D_MODEL), F32), pltpu.VMEM((tm, D_MODEL), BF16),
                        pltpu.VMEM((tm, D_MODEL), F32)],
        compiler_params=_params("arbitrary", "arbitrary"),
        name="out_projection_ffn",
    )(x, mix_h, mix_m, mod_l, mod_l, mod_l, mod_l, norm2_w, w_out, w_out, w_gate, w_up, w_down,
      final_norm_w)


def _cast_kernel(x_ref, o_ref):
    n_in = x_ref.shape[1]
    n_out = o_ref.shape[1]
    if n_out == n_in:
        o_ref[...] = x_ref[...].astype(o_ref.dtype)
    else:
        n_full = n_in - n_in % D_HEAD
        o_ref[:, :n_full] = x_ref[:, :n_full].astype(o_ref.dtype)
        o_ref[:, n_full:] = jnp.zeros((x_ref.shape[0], n_out - n_full), o_ref.dtype)
        o_ref[:, n_full:n_in] = x_ref[:, n_full:n_in].astype(o_ref.dtype)


def _to_bf16(w, n_out=None):
    depth, rows, cols = w.shape
    n_out = n_out or cols
    tr = 512
    flat = w.reshape(depth * rows, cols)
    out = pl.pallas_call(
        _cast_kernel,
        grid=(depth * rows // tr,),
        in_specs=[pl.BlockSpec((tr, cols), lambda i: (i, 0))],
        out_specs=pl.BlockSpec((tr, n_out), lambda i: (i, 0)),
        out_shape=jax.ShapeDtypeStruct((depth * rows, n_out), BF16),
        compiler_params=_params("arbitrary"),
        name="weights_to_bf16",
    )(flat)
    return out.reshape(depth, rows, n_out)


def kernel(x_prompt, x_sample, state_hgrn, state_mlstm_c, state_mlstm_n, state_mlstm_m, c, c_ctx,
           norm1_w, norm2_w, w_mod, b_mod, w_in, conv_w, conv_b, ml_gate_b, hg_lb_logits,
           hg_norm_w, ml_norm_w, w_out, w_gate, w_up, w_down, final_norm_w):
    depth = w_in.shape[0]
    n_ctx, ctx_len, _ = x_prompt.shape
    n_lat, lat_len, _ = x_sample.shape
    assert 1 + n_lat <= N_MOD_ROWS and ctx_len % SCAN_BLOCK == 0 and lat_len % SCAN_BLOCK == 0

    hg_tables = _hgrn_tables()
    ml_tables = _mlstm_tables()
    lb_all = _lower_bounds(hg_lb_logits)
    cond = jnp.zeros((N_MOD_ROWS, D_MODEL), F32).at[0].set(c_ctx).at[1:1 + n_lat].set(c)
    mod = _modulation(cond, w_mod, b_mod).reshape(depth, N_MOD_ROWS, 6, 1, D_MODEL)

    tm_in, tm_post = 256, 512
    ctx_row_in = ctx_row_post = lambda i: 0
    lat_row_in = lambda i: 1 + i // (lat_len // tm_in)
    lat_row_post = lambda i: 1 + i // (lat_len // tm_post)

    xp = x_prompt.reshape(n_ctx * ctx_len, D_MODEL)
    xs = x_sample.reshape(n_lat * lat_len, D_MODEL)
    fin_w = final_norm_w.reshape(1, D_MODEL)
    hg_fin, mc_fin, mn_fin, mm_fin = [], [], [], []
    w_in_b = _to_bf16(w_in, N_PROJ_COLS)
    w_out_b, wg_b, wu_b, wd_b = _to_bf16(w_out), _to_bf16(w_gate), _to_bf16(w_up), _to_bf16(w_down)
    for l in range(depth):
        n1 = norm1_w[l].reshape(1, D_MODEL)
        n2 = norm2_w[l].reshape(1, D_MODEL)
        taps = conv_w[l].reshape(9, 2 * N_HEADS, 1, D_HEAD)
        cbias = conv_b[l].reshape(2 * N_HEADS, 1, D_HEAD)
        gbias = jnp.pad(ml_gate_b[l], (0, D_HEAD - N_GATE_COLS)).reshape(1, D_HEAD)
        hg_nw = hg_norm_w[l].reshape(N_HEADS, 1, D_HEAD)
        ml_nw = ml_norm_w[l].reshape(N_HEADS, 1, D_HEAD)
        last = l == depth - 1

        proj = _in_projection(xp, mod[l], n1, w_in_b, l, ctx_row_in, tm_in)
        mix_h, s_h = _hgrn_mixer(proj, lb_all[l], hg_nw, None, n_ctx, ctx_len, True, hg_tables)
        mix_m, s_m = _mlstm_mixer(proj, taps, cbias, gbias, ml_nw, None, n_ctx, ctx_len, ctx_len,
                                  True, ml_tables)
        xp = _post_mixer(xp, mix_h, mix_m, mod[l], n2, w_out_b, wg_b, wu_b, wd_b, fin_w, l,
                         ctx_row_post, tm_post, last)
        hg_fin.append(s_h)
        mc_fin.append(s_m[0])
        mn_fin.append(s_m[1][..., 0])
        mm_fin.append(s_m[2][..., 0, 0])

        cached = (state_mlstm_c[:, l].astype(F32), state_mlstm_n[:, l].astype(F32)[..., None],
                  jnp.broadcast_to(state_mlstm_m[:, l].astype(F32)[..., None, None],
                                   (n_lat, 2, N_HEADS, 1, D_HEAD)))
        proj = _in_projection(xs, mod[l], n1, w_in_b, l, lat_row_in, tm_in)
        mix_h, _ = _hgrn_mixer(proj, lb_all[l], hg_nw, state_hgrn[:, l].astype(F32), n_lat, lat_len,
                               False, hg_tables)
        mix_m, _ = _mlstm_mixer(proj, taps, cbias, gbias, ml_nw, cached, n_lat, lat_len,
                                LATENT_GRID_W, False, ml_tables)
        xs = _post_mixer(xs, mix_h, mix_m, mod[l], n2, w_out_b, wg_b, wu_b, wd_b, fin_w, l,
                         lat_row_post, tm_post, last)

    return (xp.reshape(x_prompt.shape), xs.reshape(x_sample.shape), jnp.stack(hg_fin, axis=1),
            jnp.stack(mc_fin, axis=1), jnp.stack(mn_fin, axis=1), jnp.stack(mm_fin, axis=1))
```

```python
import functools

import numpy as np
import jax
import jax.numpy as jnp
from jax import lax
from jax.experimental import pallas as pl
from jax.experimental.pallas import tpu as pltpu

F32 = jnp.float32
BF16 = jnp.bfloat16

D_MODEL = 1024
N_HEADS = 4
D_HEAD = 128
GROUP_WIDTH = N_HEADS * D_HEAD
D_FF = 2816
LATENT_GRID_W = 64
EPS = 1e-6
N_MOD_ROWS = 16
N_MAIN_COLS = 9 * GROUP_WIDTH
N_GATE_COLS = 4 * N_HEADS
N_PROJ_COLS = N_MAIN_COLS + D_HEAD
SCAN_BLOCK = 128
HGRN_BLOCK = 128
LEVEL_HALVES = tuple(HGRN_BLOCK >> (i + 1) for i in range(HGRN_BLOCK.bit_length() - 1))
LOG2_E = 1.4426950408889634
CAST_ROWS = 512

VMEM_LIMIT_BYTES = 56 * 1024 * 1024


def _params(*semantics):
    return pltpu.CompilerParams(dimension_semantics=semantics, vmem_limit_bytes=VMEM_LIMIT_BYTES)


def _sigmoid(x):
    return 1.0 / (1.0 + jnp.exp(-x))


def _silu(x):
    return x * _sigmoid(x)


def _log_sigmoid(x):
    return jnp.minimum(x, 0.0) - jnp.log(1.0 + jnp.exp(-jnp.abs(x)))


def _dot(a, b):
    return jnp.dot(a, b, preferred_element_type=F32)


def _dot_nt(a, b):
    return lax.dot_general(a, b, (((1,), (1,)), ((), ())), preferred_element_type=F32)


def _split_bf16(x):
    hi = x.astype(BF16)
    lo = (x - hi.astype(F32)).astype(BF16)
    return jnp.concatenate([hi, lo], axis=1)


def _block_rows(blk, size=SCAN_BLOCK):
    return pl.ds(pl.multiple_of(blk * size, size), size)


def _hgrn_tables():
    c = HGRN_BLOCK
    t = np.arange(c)[:, None]
    u = np.arange(c)[None, :]
    scan = np.stack([u <= t, u >= t]).astype(np.float32)
    masks = np.zeros((2, 1 + len(LEVEL_HALVES), c, c), np.float32)
    masks[:, 0] = t == u
    for i, m in enumerate(LEVEL_HALVES):
        upper_t = (t % (2 * m)) >= m
        upper_u = (u % (2 * m)) >= m
        same = (t // (2 * m)) == (u // (2 * m))
        masks[0, 1 + i] = same & upper_t & ~upper_u
        masks[1, 1 + i] = same & ~upper_t & upper_u
    return jnp.asarray(scan, BF16), jnp.asarray(masks, BF16)


def _mlstm_tables():
    c = SCAN_BLOCK
    t = np.arange(c)[:, None]
    u = np.arange(c)[None, :]
    tri = np.stack([u <= t, u >= t]).astype(np.float32)
    strict = np.stack([t > u, t < u]).astype(np.float32)
    sel = np.zeros((N_HEADS, 2, 2 * D_HEAD, 2 * D_HEAD), np.float32)
    for h in range(N_HEADS):
        for d in range(2):
            for part in range(2):
                sel[h, d, part * D_HEAD + d * N_HEADS + h, :D_HEAD] = 1.0
                sel[h, d, part * D_HEAD + (2 + d) * N_HEADS + h, D_HEAD:] = 1.0
    return (jnp.asarray(tri, BF16), jnp.asarray(strict, F32),
            jnp.asarray(np.eye(c, dtype=np.float32)), jnp.asarray(sel, BF16))


def _lower_bound_kernel(logit_ref, lb_ref):
    z = logit_ref[...]
    z = z - jnp.max(z, axis=0, keepdims=True)
    e = jnp.exp(z)
    p = e / jnp.sum(e, axis=0, keepdims=True)
    depth = z.shape[0]
    run = p[0:1]
    first = run
    for l in range(depth):
        if l > 0:
            run = run + p[l:l + 1]
        lb_ref[l:l + 1, :] = run - first


def _lower_bounds(hg_lb_logits):
    depth = hg_lb_logits.shape[0]
    flat = hg_lb_logits.astype(F32).reshape(depth, 2 * GROUP_WIDTH)
    lb = pl.pallas_call(
        _lower_bound_kernel,
        out_shape=jax.ShapeDtypeStruct(flat.shape, F32),
        name="hgrn_lower_bounds",
    )(flat)
    return lb.reshape(depth, 2, N_HEADS, 1, D_HEAD)


def _mod_kernel(cond_ref, w_ref, b_ref, o_ref):
    c = cond_ref[...]
    o_ref[0] = _dot(_silu(c).astype(BF16), w_ref[0].astype(BF16)) + b_ref[0]


def _modulation(cond, w_mod, b_mod):
    depth = w_mod.shape[0]
    n_out = w_mod.shape[2]
    tn = 1536
    return pl.pallas_call(
        _mod_kernel,
        grid=(depth, n_out // tn),
        in_specs=[
            pl.BlockSpec((N_MOD_ROWS, D_MODEL), lambda l, j: (0, 0)),
            pl.BlockSpec((1, D_MODEL, tn), lambda l, j: (l, 0, j)),
            pl.BlockSpec((1, 1, tn), lambda l, j: (l, 0, j)),
        ],
        out_specs=pl.BlockSpec((1, N_MOD_ROWS, tn), lambda l, j: (l, 0, j)),
        out_shape=jax.ShapeDtypeStruct((depth, N_MOD_ROWS, n_out), F32),
        compiler_params=_params("arbitrary", "arbitrary"),
        name="adaln_modulation",
    )(cond, w_mod, b_mod.reshape(depth, 1, n_out))


def _rmsnorm(x, w):
    return x * lax.rsqrt(jnp.mean(x * x, axis=-1, keepdims=True) + EPS) * w


def _inproj_kernel(x_ref, sc_ref, sh_ref, nw_ref, w_ref, o_ref):
    h = _rmsnorm(x_ref[...], nw_ref[...]) * (1.0 + sc_ref[...]) + sh_ref[...]
    hb = h.astype(BF16)
    for n0 in range(0, N_PROJ_COLS, GROUP_WIDTH):
        n1 = min(n0 + GROUP_WIDTH, N_PROJ_COLS)
        o_ref[:, n0:n1] = _dot(hb, w_ref[:, n0:n1])


def _mod_spec(which, row_of_tile):
    return pl.BlockSpec((None, None, 1, D_MODEL), lambda i, *_: (row_of_tile(i), which, 0, 0))


def _in_projection(x, mod_l, norm_w, w_in, layer, row_of_tile, tm):
    n_tok = x.shape[0]
    return pl.pallas_call(
        _inproj_kernel,
        grid=(n_tok // tm,),
        in_specs=[
            pl.BlockSpec((tm, D_MODEL), lambda i: (i, 0)),
            _mod_spec(1, row_of_tile),
            _mod_spec(0, row_of_tile),
            pl.BlockSpec((1, D_MODEL), lambda i: (0, 0)),
            pl.BlockSpec((None, D_MODEL, N_PROJ_COLS), lambda i: (layer, 0, 0),
                         pipeline_mode=pl.Buffered(1)),
        ],
        out_specs=pl.BlockSpec((tm, N_PROJ_COLS), lambda i: (i, 0)),
        out_shape=jax.ShapeDtypeStruct((n_tok, N_PROJ_COLS), F32),
        compiler_params=_params("arbitrary"),
        name="norm_in_projection",
    )(x, mod_l, mod_l, norm_w, w_in)


def _hgrn_block(d, q, v, fz, lb, scan_ref, masks_ref, b_scr):
    c = HGRN_BLOCK
    e = jnp.exp(-jnp.abs(fz))
    inv = 1.0 / (1.0 + e)
    pos = fz >= 0.0
    k = (1.0 - lb) * (jnp.where(pos, e, 1.0) * inv)
    f = lb + (1.0 - lb) * (jnp.where(pos, 1.0, e) * inv)
    log2f = jnp.where(lb > 0.0, jnp.log2(f), jnp.minimum(fz, 0.0) * LOG2_E - jnp.log2(1.0 + e))

    b2 = _dot(scan_ref[d], _split_bf16(log2f))
    b = b2[:, :D_HEAD] + b2[:, D_HEAD:]
    total_row = c - 1 if d == 0 else 0
    total = b[total_row:total_row + 1]
    decay_in = jnp.exp2(b)
    decay_out = jnp.exp2(total - b)
    vb = v.astype(BF16)

    b_scr[...] = b

    scores = _dot_nt(q.astype(BF16), k.astype(BF16)).astype(BF16) * masks_ref[d, 0]
    t_idx = lax.broadcasted_iota(jnp.int32, (c, D_HEAD), 0)
    f_prev = pltpu.roll(f, 1, 0)
    f_next = pltpu.roll(f, c - 1, 0)
    for i, m in enumerate(LEVEL_HALVES):
        if m >= 8:
            parts = []
            for g in range(c // (2 * m)):
                ref_row = b_scr[g * 2 * m + (m - 1 if d == 0 else m):
                                g * 2 * m + (m - 1 if d == 0 else m) + 1, :]
                low = slice(g * 2 * m, g * 2 * m + m)
                high = slice(g * 2 * m + m, (g + 1) * 2 * m)
                if d == 0:
                    parts += [k[low] * jnp.exp2(ref_row - b[low]), q[high] * jnp.exp2(b[high] - ref_row)]
                else:
                    parts += [q[low] * jnp.exp2(b[low] - ref_row), k[high] * jnp.exp2(ref_row - b[high])]
            z = jnp.concatenate(parts, axis=0).astype(BF16)
            scores = scores + _dot_nt(z, z).astype(BF16) * masks_ref[d, 1 + i]
            continue
        if m == 4:
            parts = []
            for g in range(c // (2 * m)):
                r = g * 2 * m + (m - 1 if d == 0 else m)
                parts.append(jnp.exp2(-jnp.abs(b[g * 2 * m:(g + 1) * 2 * m] - b_scr[r:r + 1, :])))
            e_lvl = jnp.concatenate(parts, axis=0)
        elif m == 2:
            ph = t_idx % 4
            if d == 0:
                e_lvl = jnp.where(ph == 0, f_next, jnp.where(ph == 1, 1.0,
                                  jnp.where(ph == 2, f, f * f_prev)))
            else:
                e_lvl = jnp.where(ph == 0, f * f_next, jnp.where(ph == 1, f,
                                  jnp.where(ph == 2, 1.0, f_prev)))
        else:
            e_lvl = jnp.where(t_idx % 2 == (1 - d), f, 1.0)
        s_lvl = _dot_nt((q * e_lvl).astype(BF16), (k * e_lvl).astype(BF16))
        scores = scores + s_lvl.astype(BF16) * masks_ref[d, 1 + i]

    intra = _dot(scores, vb)
    q_in = (q * decay_in).astype(BF16)
    kv_t = _dot(v.T.astype(BF16), (k * decay_out).astype(BF16))
    return intra, q_in, kv_t, jnp.exp2(total)


def _hgrn_kernel(*refs, seq_len, has_state, want_state):
    q_ref, ff_ref, fb_ref, v_ref, g_ref, lb_ref, nw_ref, scan_ref, masks_ref = refs[:9]
    pos = 9
    s0_ref = None
    if has_state:
        s0_ref = refs[pos]
        pos += 1
    o_ref = refs[pos]
    pos += 1
    sfin_ref = None
    if want_state:
        sfin_ref = refs[pos]
        pos += 1
    acc_ref, b_scr = refs[pos:pos + 2]

    n_blocks = seq_len // HGRN_BLOCK
    fz_refs = (ff_ref, fb_ref)

    def scan_step(j, carry):
        states = list(carry)
        for d in range(2):
            rows = _block_rows(j if d == 0 else n_blocks - 1 - j, HGRN_BLOCK)
            intra, q_in, kv_t, decay = _hgrn_block(d, _silu(q_ref[rows, :]), v_ref[rows, :],
                                                   fz_refs[d][rows, :], lb_ref[d], scan_ref,
                                                   masks_ref, b_scr.at[d])
            acc_ref[d, rows, :] = intra + _dot_nt(q_in, states[d].astype(BF16))
            states[d] = states[d] * decay + kv_t
        return tuple(states)

    if has_state:
        init = (s0_ref[0].T, s0_ref[1].T)
    else:
        init = (jnp.zeros((D_HEAD, D_HEAD), F32),) * 2
    fin_f, fin_b = lax.fori_loop(0, n_blocks, scan_step, init, unroll=min(8, n_blocks))
    if want_state:
        sfin_ref[0] = fin_f.T
        sfin_ref[1] = fin_b.T

    norm_w = nw_ref[...]

    def finish(j, _):
        rows = _block_rows(j, HGRN_BLOCK)
        o = acc_ref[0, rows, :] + acc_ref[1, rows, :]
        o = o * lax.rsqrt(jnp.mean(o * o, axis=-1, keepdims=True) + EPS) * norm_w
        o_ref[rows, :] = (o * _silu(g_ref[rows, :])).astype(o_ref.dtype)
        return 0

    lax.fori_loop(0, n_blocks, finish, 0, unroll=min(4, n_blocks))


def _hgrn_mixer(proj, lb_l, norm_w, state0, batch, seq_len, want_state, tables):
    scan, masks = tables
    has_state = state0 is not None
    n_tok = batch * seq_len

    def col(section):
        return pl.BlockSpec((seq_len, D_HEAD), lambda b, h: (b, section * N_HEADS + h))

    in_specs = [col(0), col(1), col(2), col(3), col(4),
                pl.BlockSpec((2, None, 1, D_HEAD), lambda b, h: (0, h, 0, 0)),
                pl.BlockSpec((None, 1, D_HEAD), lambda b, h: (h, 0, 0)),
                pl.BlockSpec(scan.shape, lambda b, h: (0, 0, 0)),
                pl.BlockSpec(masks.shape, lambda b, h: (0, 0, 0, 0))]
    args = [proj, proj, proj, proj, proj, lb_l, norm_w, scan, masks]
    state_spec = pl.BlockSpec((None, 2, None, D_HEAD, D_HEAD), lambda b, h: (b, 0, h, 0, 0))
    if has_state:
        in_specs.append(state_spec)
        args.append(state0)
    out_shape = [jax.ShapeDtypeStruct((n_tok, GROUP_WIDTH), BF16)]
    out_specs = [pl.BlockSpec((seq_len, D_HEAD), lambda b, h: (b, h))]
    if want_state:
        out_shape.append(jax.ShapeDtypeStruct((batch, 2, N_HEADS, D_HEAD, D_HEAD), F32))
        out_specs.append(state_spec)
    res = pl.pallas_call(
        functools.partial(_hgrn_kernel, seq_len=seq_len, has_state=has_state, want_state=want_state),
        grid=(batch, N_HEADS),
        in_specs=in_specs,
        out_specs=out_specs,
        out_shape=out_shape,
        scratch_shapes=[pltpu.VMEM((2, seq_len, D_HEAD), F32),
                        pltpu.VMEM((2, HGRN_BLOCK, D_HEAD), F32)],
        compiler_params=_params("arbitrary", "arbitrary"),
        name="hgrn2_mixer",
    )(*args)
    return res if want_state else (res[0], None)


def _dwconv_rows(x_ref, taps_ref, bias_ref, y_ref, seq_len, grid_w):
    w = grid_w
    n_rows = seq_len // w
    taps = taps_ref[...]
    bias = jnp.broadcast_to(bias_ref[...], (w, x_ref.shape[1]))
    col = lax.broadcasted_iota(jnp.int32, (w, x_ref.shape[1]), 0)
    y_ref[pl.ds(0, w), :] = jnp.zeros_like(bias)
    y_ref[pl.ds(w, w), :] = bias

    def grid_row(r, _):
        base = pl.multiple_of(r * w, w)
        x = x_ref[pl.ds(base, w), :]
        left = jnp.where(col == 0, 0.0, pltpu.roll(x, 1, 0))
        right = jnp.where(col == w - 1, 0.0, pltpu.roll(x, w - 1, 0))

        def kernel_row(i):
            return left * taps[3 * i] + x * taps[3 * i + 1] + right * taps[3 * i + 2]

        y_ref[pl.ds(base + 2 * w, w), :] = kernel_row(0) + bias
        y_ref[pl.ds(base + w, w), :] += kernel_row(1)
        y_ref[pl.ds(base, w), :] += kernel_row(2)
        return 0

    lax.fori_loop(0, n_rows, grid_row, 0)


def _mlstm_block(d, rows, conv_rows, state, m_run, yq_ref, yk_ref, v_ref, gates_ref, gate_bias,
                 tri_ref, strict_ref, eye_ref, sel_ref):
    c = SCAN_BLOCK
    q = _silu(yq_ref[conv_rows, :])
    k = _silu(yk_ref[conv_rows, :]) * (D_HEAD ** -0.5)
    v = v_ref[rows, :]
    spread = _dot(_split_bf16(gates_ref[rows, :] + gate_bias), sel_ref[d])
    log_i = spread[:, :D_HEAD]
    log_f = _log_sigmoid(spread[:, D_HEAD:])

    tri = tri_ref[d]
    vis = (strict_ref[d] + eye_ref[...]) > 0.5
    mix = log_f * strict_ref[d] + log_i * eye_ref[...]
    d2 = _dot(tri, _split_bf16(mix))
    d_log = d2[:, :c] + d2[:, c:]
    b2 = _dot(tri, _split_bf16(log_f))
    b_run = b2[:, :D_HEAD] + b2[:, D_HEAD:]

    inter_log = b_run + m_run
    d_max = jnp.max(jnp.where(vis, d_log, -jnp.inf), axis=1, keepdims=True)
    m_t = jnp.maximum(inter_log, jnp.broadcast_to(d_max, (c, D_HEAD)))
    p = jnp.where(vis, jnp.exp(d_log - m_t), 0.0)
    scores = _dot_nt(q.astype(BF16), k.astype(BF16)) * p
    v_aug = jnp.concatenate([v, jnp.ones((c, D_HEAD), F32)], axis=1).astype(BF16)
    intra = _dot(scores.astype(BF16), v_aug)
    inter = _dot(q.astype(BF16), state.astype(BF16))
    a_inter = jnp.exp(inter_log - m_t)
    num = intra[:, :D_HEAD] + a_inter * inter[:, :D_HEAD]
    den = intra[:, D_HEAD:] + a_inter * inter[:, D_HEAD:]
    out = num / jnp.maximum(jnp.abs(den), jnp.exp(-m_t))

    last_row = c - 1 if d == 0 else 0
    b_last = b_run[last_row:last_row + 1]
    w_end = b_last - b_run + log_i
    m_loc = jnp.max(w_end, axis=0, keepdims=True)
    p_end = jnp.exp(w_end - m_loc)
    kv = _dot((k * p_end).T.astype(BF16), v_aug)
    m_new = jnp.maximum(b_last + m_run, m_loc)
    a = jnp.exp(b_last + m_run - m_new)
    g = jnp.exp(m_loc - m_new)
    new_state = (jnp.concatenate([a, a], axis=1) * state + jnp.concatenate([g, g], axis=1) * kv)
    return out, new_state, m_new


def _mlstm_kernel(*refs, seq_len, grid_w, has_state, want_state):
    (q_ref, k_ref, v_ref, og_ref, gates_ref, qtap_ref, ktap_ref, qb_ref, kb_ref, gb_ref, nw_ref,
     tri_ref, strict_ref, eye_ref, sel_ref) = refs[:15]
    pos = 15
    c0_ref = n0_ref = m0_ref = None
    if has_state:
        c0_ref, n0_ref, m0_ref = refs[pos:pos + 3]
        pos += 3
    o_ref = refs[pos]
    pos += 1
    cfin_ref = nfin_ref = mfin_ref = None
    if want_state:
        cfin_ref, nfin_ref, mfin_ref = refs[pos:pos + 3]
        pos += 3
    yq_ref, yk_ref, accf_ref, accb_ref = refs[pos:pos + 4]

    n_blocks = seq_len // SCAN_BLOCK
    _dwconv_rows(q_ref, qtap_ref, qb_ref, yq_ref, seq_len, grid_w)
    _dwconv_rows(k_ref, ktap_ref, kb_ref, yk_ref, seq_len, grid_w)

    gate_bias = gb_ref[...]
    conv_align = int(np.gcd(grid_w, SCAN_BLOCK))

    def conv_rows(blk):
        return pl.ds(pl.multiple_of(blk * SCAN_BLOCK + grid_w, conv_align), SCAN_BLOCK)

    def scan_step(j, carry):
        st_f, m_f, st_b, m_b = carry
        blk_b = n_blocks - 1 - j
        rows_f = _block_rows(j)
        rows_b = _block_rows(blk_b)
        out_f, st_f, m_f = _mlstm_block(0, rows_f, conv_rows(j), st_f, m_f, yq_ref, yk_ref, v_ref,
                                        gates_ref, gate_bias, tri_ref, strict_ref, eye_ref, sel_ref)
        out_b, st_b, m_b = _mlstm_block(1, rows_b, conv_rows(blk_b), st_b, m_b, yq_ref, yk_ref,
                                        v_ref, gates_ref, gate_bias, tri_ref, strict_ref, eye_ref,
                                        sel_ref)
        accf_ref[rows_f, :] = out_f
        accb_ref[rows_b, :] = out_b
        return st_f, m_f, st_b, m_b

    init = []
    for d in range(2):
        if has_state:
            n_rep = jnp.broadcast_to(n0_ref[d], (D_HEAD, D_HEAD))
            init += [jnp.concatenate([c0_ref[d], n_rep], axis=1), m0_ref[d]]
        else:
            init += [jnp.zeros((D_HEAD, 2 * D_HEAD), F32), jnp.zeros((1, D_HEAD), F32)]
    fin = lax.fori_loop(0, n_blocks, scan_step, tuple(init), unroll=min(8, n_blocks))
    if want_state:
        for d in range(2):
            cfin_ref[d] = fin[2 * d][:, :D_HEAD]
            nfin_ref[d] = fin[2 * d][:, D_HEAD:D_HEAD + 1]
            mfin_ref[d] = fin[2 * d + 1]

    norm_w = nw_ref[...]

    def finish(j, _):
        rows = _block_rows(j)
        o = accf_ref[rows, :] + accb_ref[rows, :]
        o = o * lax.rsqrt(jnp.mean(o * o, axis=-1, keepdims=True) + EPS) * norm_w
        o_ref[rows, :] = (o * _sigmoid(og_ref[rows, :])).astype(o_ref.dtype)
        return 0

    lax.fori_loop(0, n_blocks, finish, 0, unroll=min(4, n_blocks))


def _mlstm_mixer(proj, conv_taps, conv_bias, gate_bias, norm_w, states0, batch, seq_len, grid_w,
                 want_state, tables):
    tri, strict, eye, sel = tables
    has_state = states0 is not None
    n_tok = batch * seq_len

    def col(block_index):
        return pl.BlockSpec((seq_len, D_HEAD), lambda b, h: (b, block_index(h)))

    def const(arr):
        return pl.BlockSpec(arr.shape, lambda b, h: (0,) * arr.ndim)

    def per_head(offset):
        return pl.BlockSpec((None, 1, D_HEAD), lambda b, h: (offset + h, 0, 0))

    in_specs = [col(lambda h: 5 * N_HEADS + h), col(lambda h: 6 * N_HEADS + h),
                col(lambda h: 7 * N_HEADS + h), col(lambda h: 8 * N_HEADS + h),
                col(lambda h: 9 * N_HEADS),
                pl.BlockSpec((9, None, 1, D_HEAD), lambda b, h: (0, h, 0, 0)),
                pl.BlockSpec((9, None, 1, D_HEAD), lambda b, h: (0, N_HEADS + h, 0, 0)),
                per_head(0), per_head(N_HEADS), const(gate_bias), per_head(0),
                const(tri), const(strict), const(eye),
                pl.BlockSpec((None,) + sel.shape[1:], lambda b, h: (h, 0, 0, 0))]
    args = [proj, proj, proj, proj, proj, conv_taps, conv_taps, conv_bias, conv_bias, gate_bias,
            norm_w, tri, strict, eye, sel]

    def state_spec(rows, lanes):
        return pl.BlockSpec((None, 2, None, rows, lanes), lambda b, h: (b, 0, h, 0, 0))

    state_specs = [state_spec(D_HEAD, D_HEAD), state_spec(D_HEAD, 1), state_spec(1, D_HEAD)]
    if has_state:
        in_specs += state_specs
        args += list(states0)
    out_shape = [jax.ShapeDtypeStruct((n_tok, GROUP_WIDTH), BF16)]
    out_specs = [pl.BlockSpec((seq_len, D_HEAD), lambda b, h: (b, h))]
    if want_state:
        out_shape += [jax.ShapeDtypeStruct((batch, 2, N_HEADS, D_HEAD, D_HEAD), F32),
                      jax.ShapeDtypeStruct((batch, 2, N_HEADS, D_HEAD, 1), F32),
                      jax.ShapeDtypeStruct((batch, 2, N_HEADS, 1, D_HEAD), F32)]
        out_specs += state_specs
    res = pl.pallas_call(
        functools.partial(_mlstm_kernel, seq_len=seq_len, grid_w=grid_w, has_state=has_state,
                          want_state=want_state),
        grid=(batch, N_HEADS),
        in_specs=in_specs,
        out_specs=out_specs,
        out_shape=out_shape,
        scratch_shapes=[pltpu.VMEM((seq_len + 2 * grid_w, D_HEAD), F32)] * 2
        + [pltpu.VMEM((seq_len, D_HEAD), F32)] * 2,
        compiler_params=_params("arbitrary", "arbitrary"),
        name="mlstm_mixer",
    )(*args)
    return (res[0], tuple(res[1:])) if want_state else (res[0], None)


def _post_kernel(x_ref, mh_ref, mm_ref, g1_ref, sc2_ref, sh2_ref, g2_ref, n2_ref, woh_ref, wom_ref,
                 wg_ref, wu_ref, wd_ref, fn_ref, o_ref, *, final_norm):
    mixed = _dot(mh_ref[...], woh_ref[...]) + _dot(mm_ref[...], wom_ref[...])
    x1 = x_ref[...] + g1_ref[...] * mixed
    hb = (_rmsnorm(x1, n2_ref[...]) * (1.0 + sc2_ref[...]) + sh2_ref[...]).astype(BF16)
    act = _silu(_dot(hb, wg_ref[...])) * _dot(hb, wu_ref[...])
    x2 = x1 + g2_ref[...] * _dot(act.astype(BF16), wd_ref[...])
    if final_norm:
        x2 = _rmsnorm(x2, fn_ref[...])
    o_ref[...] = x2


def _post_mixer(x, mix_h, mix_m, mod_l, norm2_w, w_out, w_gate, w_up, w_down, final_norm_w, layer,
                row_of_tile, tm, final_norm):
    n_tok = x.shape[0]
    row = lambda i: (i, 0)
    fixed = lambda i: (0, 0)

    def resident(shape, index_map):
        return pl.BlockSpec(shape, index_map, pipeline_mode=pl.Buffered(1))

    return pl.pallas_call(
        functools.partial(_post_kernel, final_norm=final_norm),
        grid=(n_tok // tm,),
        in_specs=[
            pl.BlockSpec((tm, D_MODEL), row),
            pl.BlockSpec((tm, GROUP_WIDTH), row),
            pl.BlockSpec((tm, GROUP_WIDTH), row),
            _mod_spec(2, row_of_tile), _mod_spec(4, row_of_tile), _mod_spec(3, row_of_tile),
            _mod_spec(5, row_of_tile),
            pl.BlockSpec((1, D_MODEL), fixed),
            resident((None, GROUP_WIDTH, D_MODEL), lambda i: (layer, 0, 0)),
            resident((None, GROUP_WIDTH, D_MODEL), lambda i: (layer, 1, 0)),
            resident((None, D_MODEL, D_FF), lambda i: (layer, 0, 0)),
            resident((None, D_MODEL, D_FF), lambda i: (layer, 0, 0)),
            resident((None, D_FF, D_MODEL), lambda i: (layer, 0, 0)),
            pl.BlockSpec((1, D_MODEL), fixed),
        ],
        out_specs=pl.BlockSpec((tm, D_MODEL), row),
        out_shape=jax.ShapeDtypeStruct((n_tok, D_MODEL), F32),
        compiler_params=_params("arbitrary"),
        name="out_projection_ffn",
    )(x, mix_h, mix_m, mod_l, mod_l, mod_l, mod_l, norm2_w, w_out, w_out, w_gate, w_up, w_down,
      final_norm_w)


def _cast_kernel(x_ref, o_ref):
    n_in = x_ref.shape[1]
    n_out = o_ref.shape[1]
    if n_out == n_in:
        o_ref[...] = x_ref[...].astype(o_ref.dtype)
    else:
        n_full = n_in - n_in % D_HEAD
        o_ref[:, :n_full] = x_ref[:, :n_full].astype(o_ref.dtype)
        o_ref[:, n_full:] = jnp.zeros((x_ref.shape[0], n_out - n_full), o_ref.dtype)
        o_ref[:, n_full:n_in] = x_ref[:, n_full:n_in].astype(o_ref.dtype)


def _to_bf16(w, n_out=None):
    depth, rows, cols = w.shape
    n_out = n_out or cols
    tr = int(np.gcd(rows, CAST_ROWS))
    return pl.pallas_call(
        _cast_kernel,
        grid=(depth, rows // tr),
        in_specs=[pl.BlockSpec((None, tr, cols), lambda l, i: (l, i, 0))],
        out_specs=pl.BlockSpec((None, tr, n_out), lambda l, i: (l, i, 0)),
        out_shape=jax.ShapeDtypeStruct((depth, rows, n_out), BF16),
        compiler_params=_params("arbitrary", "arbitrary"),
        name="weights_to_bf16",
    )(w)


def kernel(x_prompt, x_sample, state_hgrn, state_mlstm_c, state_mlstm_n, state_mlstm_m, c, c_ctx,
           norm1_w, norm2_w, w_mod, b_mod, w_in, conv_w, conv_b, ml_gate_b, hg_lb_logits,
           hg_norm_w, ml_norm_w, w_out, w_gate, w_up, w_down, final_norm_w):
    depth = w_in.shape[0]
    n_ctx, ctx_len, _ = x_prompt.shape
    n_lat, lat_len, _ = x_sample.shape
    assert 1 + n_lat <= N_MOD_ROWS and ctx_len % SCAN_BLOCK == 0 and lat_len % SCAN_BLOCK == 0

    hg_tables = _hgrn_tables()
    ml_tables = _mlstm_tables()
    lb_all = _lower_bounds(hg_lb_logits)
    cond = jnp.zeros((N_MOD_ROWS, D_MODEL), F32).at[0].set(c_ctx).at[1:1 + n_lat].set(c)
    mod = _modulation(cond, w_mod, b_mod).reshape(depth, N_MOD_ROWS, 6, 1, D_MODEL)

    tm_in, tm_post = 512, 512
    ctx_row_in = ctx_row_post = lambda i: 0
    lat_row_in = lambda i: 1 + i // (lat_len // tm_in)
    lat_row_post = lambda i: 1 + i // (lat_len // tm_post)

    xp = x_prompt.reshape(n_ctx * ctx_len, D_MODEL)
    xs = x_sample.reshape(n_lat * lat_len, D_MODEL)
    fin_w = final_norm_w.reshape(1, D_MODEL)
    hg_fin, mc_fin, mn_fin, mm_fin = [], [], [], []
    w_in_b = _to_bf16(w_in, N_PROJ_COLS)
    w_out_b, wg_b, wu_b, wd_b = _to_bf16(w_out), _to_bf16(w_gate), _to_bf16(w_up), _to_bf16(w_down)
    for l in range(depth):
        n1 = norm1_w[l].reshape(1, D_MODEL)
        n2 = norm2_w[l].reshape(1, D_MODEL)
        taps = conv_w[l].reshape(9, 2 * N_HEADS, 1, D_HEAD)
        cbias = conv_b[l].reshape(2 * N_HEADS, 1, D_HEAD)
        gbias = jnp.pad(ml_gate_b[l], (0, D_HEAD - N_GATE_COLS)).reshape(1, D_HEAD)
        hg_nw = hg_norm_w[l].reshape(N_HEADS, 1, D_HEAD)
        ml_nw = ml_norm_w[l].reshape(N_HEADS, 1, D_HEAD)
        last = l == depth - 1

        proj = _in_projection(xp, mod[l], n1, w_in_b, l, ctx_row_in, tm_in)
        mix_h, s_h = _hgrn_mixer(proj, lb_all[l], hg_nw, None, n_ctx, ctx_len, True, hg_tables)
        mix_m, s_m = _mlstm_mixer(proj, taps, cbias, gbias, ml_nw, None, n_ctx, ctx_len, ctx_len,
                                  True, ml_tables)
        xp = _post_mixer(xp, mix_h, mix_m, mod[l], n2, w_out_b, wg_b, wu_b, wd_b, fin_w, l,
                         ctx_row_post, tm_post, last)
        hg_fin.append(s_h)
        mc_fin.append(s_m[0])
        mn_fin.append(s_m[1][..., 0])
        mm_fin.append(s_m[2][..., 0, 0])

        cached = (state_mlstm_c[:, l].astype(F32), state_mlstm_n[:, l].astype(F32)[..., None],
                  jnp.broadcast_to(state_mlstm_m[:, l].astype(F32)[..., None, None],
                                   (n_lat, 2, N_HEADS, 1, D_HEAD)))
        proj = _in_projection(xs, mod[l], n1, w_in_b, l, lat_row_in, tm_in)
        mix_h, _ = _hgrn_mixer(proj, lb_all[l], hg_nw, state_hgrn[:, l].astype(F32), n_lat, lat_len,
                               False, hg_tables)
        mix_m, _ = _mlstm_mixer(proj, taps, cbias, gbias, ml_nw, cached, n_lat, lat_len,
                                LATENT_GRID_W, False, ml_tables)
        xs = _post_mixer(xs, mix_h, mix_m, mod[l], n2, w_out_b, wg_b, wu_b, wd_b, fin_w, l,
                         lat_row_post, tm_post, last)

    return (xp.reshape(x_prompt.shape), xs.reshape(x_sample.shape), jnp.stack(hg_fin, axis=1),
            jnp.stack(mc_fin, axis=1), jnp.stack(mn_fin, axis=1), jnp.stack(mm_fin, axis=1))
```

```python
import functools

import numpy as np
import jax
import jax.numpy as jnp
from jax import lax
from jax.experimental import pallas as pl
from jax.experimental.pallas import tpu as pltpu

F32 = jnp.float32
BF16 = jnp.bfloat16

D_MODEL = 1024
N_HEADS = 4
D_HEAD = 128
GROUP_WIDTH = N_HEADS * D_HEAD
D_FF = 2816
LATENT_GRID_W = 64
EPS = 1e-6
N_MOD_ROWS = 16
N_MAIN_COLS = 9 * GROUP_WIDTH
N_GATE_COLS = 4 * N_HEADS
N_PROJ_COLS = N_MAIN_COLS + D_HEAD
SCAN_BLOCK = 128
HGRN_BLOCK = 128
LEVEL_HALVES = tuple(HGRN_BLOCK >> (i + 1) for i in range(HGRN_BLOCK.bit_length() - 1))
LOG2_E = 1.4426950408889634
CAST_ROWS = 512
CTX_HEADS_PER_STEP = 4
LATENT_HEADS_PER_STEP = 1

VMEM_LIMIT_BYTES = 56 * 1024 * 1024


def _params(*semantics):
    return pltpu.CompilerParams(dimension_semantics=semantics, vmem_limit_bytes=VMEM_LIMIT_BYTES)


def _sigmoid(x):
    return 1.0 / (1.0 + jnp.exp(-x))


def _silu(x):
    return x * _sigmoid(x)


def _log_sigmoid(x):
    return jnp.minimum(x, 0.0) - jnp.log(1.0 + jnp.exp(-jnp.abs(x)))


def _dot(a, b):
    return jnp.dot(a, b, preferred_element_type=F32)


def _dot_nt(a, b):
    return lax.dot_general(a, b, (((1,), (1,)), ((), ())), preferred_element_type=F32)


def _split_bf16(x):
    hi = x.astype(BF16)
    lo = (x - hi.astype(F32)).astype(BF16)
    return jnp.concatenate([hi, lo], axis=1)


def _block_rows(blk, size=SCAN_BLOCK):
    return pl.ds(pl.multiple_of(blk * size, size), size)


def _hgrn_tables():
    c = HGRN_BLOCK
    t = np.arange(c)[:, None]
    u = np.arange(c)[None, :]
    scan = np.stack([u <= t, u >= t]).astype(np.float32)
    masks = np.zeros((2, 1 + len(LEVEL_HALVES), c, c), np.float32)
    masks[:, 0] = t == u
    for i, m in enumerate(LEVEL_HALVES):
        upper_t = (t % (2 * m)) >= m
        upper_u = (u % (2 * m)) >= m
        same = (t // (2 * m)) == (u // (2 * m))
        masks[0, 1 + i] = same & upper_t & ~upper_u
        masks[1, 1 + i] = same & ~upper_t & upper_u
    return jnp.asarray(scan, BF16), jnp.asarray(masks, BF16)


def _mlstm_tables():
    c = SCAN_BLOCK
    t = np.arange(c)[:, None]
    u = np.arange(c)[None, :]
    tri = np.stack([u <= t, u >= t]).astype(np.float32)
    strict = np.stack([t > u, t < u]).astype(np.float32)
    sel = np.zeros((N_HEADS, 2, 2 * D_HEAD, 2 * D_HEAD), np.float32)
    for h in range(N_HEADS):
        for d in range(2):
            for part in range(2):
                sel[h, d, part * D_HEAD + d * N_HEADS + h, :D_HEAD] = 1.0
                sel[h, d, part * D_HEAD + (2 + d) * N_HEADS + h, D_HEAD:] = 1.0
    return (jnp.asarray(tri, BF16), jnp.asarray(strict, F32),
            jnp.asarray(np.eye(c, dtype=np.float32)), jnp.asarray(sel, BF16))


def _lower_bound_kernel(logit_ref, lb_ref):
    z = logit_ref[...]
    z = z - jnp.max(z, axis=0, keepdims=True)
    e = jnp.exp(z)
    p = e / jnp.sum(e, axis=0, keepdims=True)
    depth = z.shape[0]
    run = p[0:1]
    first = run
    for l in range(depth):
        if l > 0:
            run = run + p[l:l + 1]
        lb_ref[l:l + 1, :] = run - first


def _lower_bounds(hg_lb_logits):
    depth = hg_lb_logits.shape[0]
    flat = hg_lb_logits.astype(F32).reshape(depth, 2 * GROUP_WIDTH)
    lb = pl.pallas_call(
        _lower_bound_kernel,
        out_shape=jax.ShapeDtypeStruct(flat.shape, F32),
        name="hgrn_lower_bounds",
    )(flat)
    return lb.reshape(depth, 2, N_HEADS, 1, D_HEAD)


def _mod_kernel(cond_ref, w_ref, b_ref, o_ref):
    c = cond_ref[...]
    o_ref[0] = _dot(_silu(c).astype(BF16), w_ref[0].astype(BF16)) + b_ref[0]


def _modulation(cond, w_mod, b_mod):
    depth = w_mod.shape[0]
    n_out = w_mod.shape[2]
    tn = 1536
    return pl.pallas_call(
        _mod_kernel,
        grid=(depth, n_out // tn),
        in_specs=[
            pl.BlockSpec((N_MOD_ROWS, D_MODEL), lambda l, j: (0, 0)),
            pl.BlockSpec((1, D_MODEL, tn), lambda l, j: (l, 0, j)),
            pl.BlockSpec((1, 1, tn), lambda l, j: (l, 0, j)),
        ],
        out_specs=pl.BlockSpec((1, N_MOD_ROWS, tn), lambda l, j: (l, 0, j)),
        out_shape=jax.ShapeDtypeStruct((depth, N_MOD_ROWS, n_out), F32),
        compiler_params=_params("arbitrary", "arbitrary"),
        name="adaln_modulation",
    )(cond, w_mod, b_mod.reshape(depth, 1, n_out))


def _rmsnorm(x, w):
    return x * lax.rsqrt(jnp.mean(x * x, axis=-1, keepdims=True) + EPS) * w


def _inproj_kernel(x_ref, sc_ref, sh_ref, nw_ref, w_ref, o_ref):
    h = _rmsnorm(x_ref[...], nw_ref[...]) * (1.0 + sc_ref[...]) + sh_ref[...]
    hb = h.astype(BF16)
    for n0 in range(0, N_PROJ_COLS, GROUP_WIDTH):
        n1 = min(n0 + GROUP_WIDTH, N_PROJ_COLS)
        res = _dot(hb, w_ref[:, n0:n1])
        for s in range((n1 - n0) // D_HEAD):
            o_ref[n0 // D_HEAD + s] = res[:, s * D_HEAD:(s + 1) * D_HEAD]


def _mod_spec(which, row_of_tile):
    return pl.BlockSpec((None, None, 1, D_MODEL), lambda i, *_: (row_of_tile(i), which, 0, 0))


def _in_projection(x, mod_l, norm_w, w_in, layer, row_of_tile, tm):
    n_tok = x.shape[0]
    return pl.pallas_call(
        _inproj_kernel,
        grid=(n_tok // tm,),
        in_specs=[
            pl.BlockSpec((tm, D_MODEL), lambda i: (i, 0)),
            _mod_spec(1, row_of_tile),
            _mod_spec(0, row_of_tile),
            pl.BlockSpec((1, D_MODEL), lambda i: (0, 0)),
            pl.BlockSpec((None, D_MODEL, N_PROJ_COLS), lambda i: (layer, 0, 0),
                         pipeline_mode=pl.Buffered(1)),
        ],
        out_specs=pl.BlockSpec((N_PROJ_COLS // D_HEAD, tm, D_HEAD), lambda i: (0, i, 0)),
        out_shape=jax.ShapeDtypeStruct((N_PROJ_COLS // D_HEAD, n_tok, D_HEAD), F32),
        compiler_params=_params("arbitrary"),
        name="norm_in_projection",
    )(x, mod_l, mod_l, norm_w, w_in)


def _hgrn_block(d, q, v, fz, lb, scan_ref, masks_ref, b_scr):
    c = HGRN_BLOCK
    e = jnp.exp(-jnp.abs(fz))
    inv = 1.0 / (1.0 + e)
    pos = fz >= 0.0
    k = (1.0 - lb) * (jnp.where(pos, e, 1.0) * inv)
    f = lb + (1.0 - lb) * (jnp.where(pos, 1.0, e) * inv)
    log2f = jnp.where(lb > 0.0, jnp.log2(f), jnp.minimum(fz, 0.0) * LOG2_E - jnp.log2(1.0 + e))

    b2 = _dot(scan_ref[d], _split_bf16(log2f))
    b = b2[:, :D_HEAD] + b2[:, D_HEAD:]
    total_row = c - 1 if d == 0 else 0
    total = b[total_row:total_row + 1]
    decay_in = jnp.exp2(b)
    decay_out = jnp.exp2(total - b)
    vb = v.astype(BF16)

    b_scr[...] = b

    scores = _dot_nt(q.astype(BF16), k.astype(BF16)).astype(BF16) * masks_ref[d, 0]
    t_idx = lax.broadcasted_iota(jnp.int32, (c, D_HEAD), 0)
    f_prev = pltpu.roll(f, 1, 0)
    f_next = pltpu.roll(f, c - 1, 0)
    for i, m in enumerate(LEVEL_HALVES):
        if m >= 8:
            parts = []
            for g in range(c // (2 * m)):
                ref_row = b_scr[g * 2 * m + (m - 1 if d == 0 else m):
                                g * 2 * m + (m - 1 if d == 0 else m) + 1, :]
                low = slice(g * 2 * m, g * 2 * m + m)
                high = slice(g * 2 * m + m, (g + 1) * 2 * m)
                if d == 0:
                    parts += [k[low] * jnp.exp2(ref_row - b[low]), q[high] * jnp.exp2(b[high] - ref_row)]
                else:
                    parts += [q[low] * jnp.exp2(b[low] - ref_row), k[high] * jnp.exp2(ref_row - b[high])]
            z = jnp.concatenate(parts, axis=0).astype(BF16)
            scores = scores + _dot_nt(z, z).astype(BF16) * masks_ref[d, 1 + i]
            continue
        if m == 4:
            parts = []
            for g in range(c // (2 * m)):
                r = g * 2 * m + (m - 1 if d == 0 else m)
                parts.append(jnp.exp2(-jnp.abs(b[g * 2 * m:(g + 1) * 2 * m] - b_scr[r:r + 1, :])))
            e_lvl = jnp.concatenate(parts, axis=0)
        elif m == 2:
            ph = t_idx % 4
            if d == 0:
                e_lvl = jnp.where(ph == 0, f_next, jnp.where(ph == 1, 1.0,
                                  jnp.where(ph == 2, f, f * f_prev)))
            else:
                e_lvl = jnp.where(ph == 0, f * f_next, jnp.where(ph == 1, f,
                                  jnp.where(ph == 2, 1.0, f_prev)))
        else:
            e_lvl = jnp.where(t_idx % 2 == (1 - d), f, 1.0)
        s_lvl = _dot_nt((q * e_lvl).astype(BF16), (k * e_lvl).astype(BF16))
        scores = scores + s_lvl.astype(BF16) * masks_ref[d, 1 + i]

    intra = _dot(scores, vb)
    q_in = (q * decay_in).astype(BF16)
    kv_t = _dot(v.T.astype(BF16), (k * decay_out).astype(BF16))
    return intra, q_in, kv_t, jnp.exp2(total)


def _hgrn_kernel(*refs, seq_len, heads, has_state, want_state):
    q_ref, ff_ref, fb_ref, v_ref, g_ref, lb_ref, nw_ref, scan_ref, masks_ref = refs[:9]
    pos = 9
    s0_ref = None
    if has_state:
        s0_ref = refs[pos]
        pos += 1
    o_ref = refs[pos]
    pos += 1
    sfin_ref = None
    if want_state:
        sfin_ref = refs[pos]
        pos += 1
    acc_ref, b_scr = refs[pos:pos + 2]
    for h in range(heads):
        _hgrn_head(q_ref.at[h], ff_ref.at[h], fb_ref.at[h], v_ref.at[h], g_ref.at[h], lb_ref.at[:, h],
                   nw_ref.at[h], scan_ref, masks_ref, s0_ref.at[:, h] if has_state else None,
                   o_ref.at[h], sfin_ref.at[:, h] if want_state else None, acc_ref, b_scr, seq_len)


def _hgrn_head(q_ref, ff_ref, fb_ref, v_ref, g_ref, lb_ref, nw_ref, scan_ref, masks_ref, s0_ref,
               o_ref, sfin_ref, acc_ref, b_scr, seq_len):
    has_state = s0_ref is not None
    want_state = sfin_ref is not None
    n_blocks = seq_len // HGRN_BLOCK
    fz_refs = (ff_ref, fb_ref)

    def scan_step(j, carry):
        states = list(carry)
        for d in range(2):
            rows = _block_rows(j if d == 0 else n_blocks - 1 - j, HGRN_BLOCK)
            intra, q_in, kv_t, decay = _hgrn_block(d, _silu(q_ref[rows, :]), v_ref[rows, :],
                                                   fz_refs[d][rows, :], lb_ref[d], scan_ref,
                                                   masks_ref, b_scr.at[d])
            acc_ref[d, rows, :] = intra + _dot_nt(q_in, states[d].astype(BF16))
            states[d] = states[d] * decay + kv_t
        return tuple(states)

    if has_state:
        init = (s0_ref[0].T, s0_ref[1].T)
    else:
        init = (jnp.zeros((D_HEAD, D_HEAD), F32),) * 2
    fin_f, fin_b = lax.fori_loop(0, n_blocks, scan_step, init, unroll=min(8, n_blocks))
    if want_state:
        sfin_ref[0] = fin_f.T
        sfin_ref[1] = fin_b.T

    norm_w = nw_ref[...]

    def finish(j, _):
        rows = _block_rows(j, HGRN_BLOCK)
        o = acc_ref[0, rows, :] + acc_ref[1, rows, :]
        o = o * lax.rsqrt(jnp.mean(o * o, axis=-1, keepdims=True) + EPS) * norm_w
        o_ref[rows, :] = (o * _silu(g_ref[rows, :])).astype(o_ref.dtype)
        return 0

    lax.fori_loop(0, n_blocks, finish, 0, unroll=min(4, n_blocks))


def _hgrn_mixer(proj, lb_l, norm_w, state0, batch, seq_len, want_state, tables, heads_per_step):
    scan, masks = tables
    has_state = state0 is not None
    n_tok = batch * seq_len

    hp = heads_per_step
    n_groups = N_HEADS // hp

    def col(section):
        return pl.BlockSpec((hp, seq_len, D_HEAD), lambda b, h: (section * n_groups + h, b, 0))

    in_specs = [col(0), col(1), col(2), col(3), col(4),
                pl.BlockSpec((2, hp, 1, D_HEAD), lambda b, h: (0, h, 0, 0)),
                pl.BlockSpec((hp, 1, D_HEAD), lambda b, h: (h, 0, 0)),
                pl.BlockSpec(scan.shape, lambda b, h: (0, 0, 0)),
                pl.BlockSpec(masks.shape, lambda b, h: (0, 0, 0, 0))]
    args = [proj, proj, proj, proj, proj, lb_l, norm_w, scan, masks]
    state_spec = pl.BlockSpec((None, 2, hp, D_HEAD, D_HEAD), lambda b, h: (b, 0, h, 0, 0))
    if has_state:
        in_specs.append(state_spec)
        args.append(state0)
    out_shape = [jax.ShapeDtypeStruct((N_HEADS, n_tok, D_HEAD), BF16)]
    out_specs = [pl.BlockSpec((hp, seq_len, D_HEAD), lambda b, h: (h, b, 0))]
    if want_state:
        out_shape.append(jax.ShapeDtypeStruct((batch, 2, N_HEADS, D_HEAD, D_HEAD), F32))
        out_specs.append(state_spec)
    res = pl.pallas_call(
        functools.partial(_hgrn_kernel, seq_len=seq_len, heads=hp, has_state=has_state,
                          want_state=want_state),
        grid=(batch, n_groups),
        in_specs=in_specs,
        out_specs=out_specs,
        out_shape=out_shape,
        scratch_shapes=[pltpu.VMEM((2, seq_len, D_HEAD), F32),
                        pltpu.VMEM((2, HGRN_BLOCK, D_HEAD), F32)],
        compiler_params=_params("arbitrary", "arbitrary"),
        name="hgrn2_mixer",
    )(*args)
    return res if want_state else (res[0], None)


def _dwconv_rows(x_ref, taps_ref, bias_ref, y_ref, seq_len, grid_w):
    w = grid_w
    n_rows = seq_len // w
    taps = taps_ref[...]
    bias = jnp.broadcast_to(bias_ref[...], (w, x_ref.shape[1]))
    col = lax.broadcasted_iota(jnp.int32, (w, x_ref.shape[1]), 0)
    y_ref[pl.ds(0, w), :] = jnp.zeros_like(bias)
    y_ref[pl.ds(w, w), :] = bias

    def grid_row(r, _):
        base = pl.multiple_of(r * w, w)
        x = x_ref[pl.ds(base, w), :]
        left = jnp.where(col == 0, 0.0, pltpu.roll(x, 1, 0))
        right = jnp.where(col == w - 1, 0.0, pltpu.roll(x, w - 1, 0))

        def kernel_row(i):
            return left * taps[3 * i] + x * taps[3 * i + 1] + right * taps[3 * i + 2]

        y_ref[pl.ds(base + 2 * w, w), :] = kernel_row(0) + bias
        y_ref[pl.ds(base + w, w), :] += kernel_row(1)
        y_ref[pl.ds(base, w), :] += kernel_row(2)
        return 0

    lax.fori_loop(0, n_rows, grid_row, 0)


def _mlstm_block(d, rows, conv_rows, state, m_run, yq_ref, yk_ref, v_ref, gates_ref, gate_bias,
                 tri_ref, strict_ref, eye_ref, sel_ref):
    c = SCAN_BLOCK
    q = _silu(yq_ref[conv_rows, :])
    k = _silu(yk_ref[conv_rows, :]) * (D_HEAD ** -0.5)
    v = v_ref[rows, :]
    spread = _dot(_split_bf16(gates_ref[rows, :] + gate_bias), sel_ref[d])
    log_i = spread[:, :D_HEAD]
    log_f = _log_sigmoid(spread[:, D_HEAD:])

    tri = tri_ref[d]
    vis = (strict_ref[d] + eye_ref[...]) > 0.5
    mix = log_f * strict_ref[d] + log_i * eye_ref[...]
    d2 = _dot(tri, _split_bf16(mix))
    d_log = d2[:, :c] + d2[:, c:]
    b2 = _dot(tri, _split_bf16(log_f))
    b_run = b2[:, :D_HEAD] + b2[:, D_HEAD:]

    inter_log = b_run + m_run
    d_max = jnp.max(jnp.where(vis, d_log, -jnp.inf), axis=1, keepdims=True)
    m_t = jnp.maximum(inter_log, jnp.broadcast_to(d_max, (c, D_HEAD)))
    p = jnp.where(vis, jnp.exp(d_log - m_t), 0.0)
    scores = _dot_nt(q.astype(BF16), k.astype(BF16)) * p
    v_aug = jnp.concatenate([v, jnp.ones((c, D_HEAD), F32)], axis=1).astype(BF16)
    intra = _dot(scores.astype(BF16), v_aug)
    inter = _dot(q.astype(BF16), state.astype(BF16))
    a_inter = jnp.exp(inter_log - m_t)
    num = intra[:, :D_HEAD] + a_inter * inter[:, :D_HEAD]
    den = intra[:, D_HEAD:] + a_inter * inter[:, D_HEAD:]
    out = num / jnp.maximum(jnp.abs(den), jnp.exp(-m_t))

    last_row = c - 1 if d == 0 else 0
    b_last = b_run[last_row:last_row + 1]
    w_end = b_last - b_run + log_i
    m_loc = jnp.max(w_end, axis=0, keepdims=True)
    p_end = jnp.exp(w_end - m_loc)
    kv = _dot((k * p_end).T.astype(BF16), v_aug)
    m_new = jnp.maximum(b_last + m_run, m_loc)
    a = jnp.exp(b_last + m_run - m_new)
    g = jnp.exp(m_loc - m_new)
    new_state = (jnp.concatenate([a, a], axis=1) * state + jnp.concatenate([g, g], axis=1) * kv)
    return out, new_state, m_new


def _mlstm_kernel(*refs, seq_len, grid_w, heads, has_state, want_state):
    (q_ref, k_ref, v_ref, og_ref, gates_ref, qtap_ref, ktap_ref, qb_ref, kb_ref, gb_ref, nw_ref,
     tri_ref, strict_ref, eye_ref, sel_ref) = refs[:15]
    pos = 15
    state_in = state_out = (None, None, None)
    if has_state:
        state_in = refs[pos:pos + 3]
        pos += 3
    o_ref = refs[pos]
    pos += 1
    if want_state:
        state_out = refs[pos:pos + 3]
        pos += 3
    scratch = refs[pos:pos + 4]
    for h in range(heads):
        _mlstm_head(q_ref.at[h], k_ref.at[h], v_ref.at[h], og_ref.at[h], gates_ref, qtap_ref.at[:, h],
                    ktap_ref.at[:, h], qb_ref.at[h], kb_ref.at[h], gb_ref, nw_ref.at[h], tri_ref,
                    strict_ref, eye_ref, sel_ref.at[h],
                    tuple(r.at[:, h] for r in state_in) if has_state else state_in,
                    o_ref.at[h],
                    tuple(r.at[:, h] for r in state_out) if want_state else state_out,
                    scratch, seq_len, grid_w)


def _mlstm_head(q_ref, k_ref, v_ref, og_ref, gates_ref, qtap_ref, ktap_ref, qb_ref, kb_ref, gb_ref,
                nw_ref, tri_ref, strict_ref, eye_ref, sel_ref, state_in, o_ref, state_out, scratch,
                seq_len, grid_w):
    c0_ref, n0_ref, m0_ref = state_in
    cfin_ref, nfin_ref, mfin_ref = state_out
    has_state = c0_ref is not None
    want_state = cfin_ref is not None
    yq_ref, yk_ref, accf_ref, accb_ref = scratch

    n_blocks = seq_len // SCAN_BLOCK
    _dwconv_rows(q_ref, qtap_ref, qb_ref, yq_ref, seq_len, grid_w)
    _dwconv_rows(k_ref, ktap_ref, kb_ref, yk_ref, seq_len, grid_w)

    gate_bias = gb_ref[...]
    conv_align = int(np.gcd(grid_w, SCAN_BLOCK))

    def conv_rows(blk):
        return pl.ds(pl.multiple_of(blk * SCAN_BLOCK + grid_w, conv_align), SCAN_BLOCK)

    def scan_step(j, carry):
        st_f, m_f, st_b, m_b = carry
        blk_b = n_blocks - 1 - j
        rows_f = _block_rows(j)
        rows_b = _block_rows(blk_b)
        out_f, st_f, m_f = _mlstm_block(0, rows_f, conv_rows(j), st_f, m_f, yq_ref, yk_ref, v_ref,
                                        gates_ref, gate_bias, tri_ref, strict_ref, eye_ref, sel_ref)
        out_b, st_b, m_b = _mlstm_block(1, rows_b, conv_rows(blk_b), st_b, m_b, yq_ref, yk_ref,
                                        v_ref, gates_ref, gate_bias, tri_ref, strict_ref, eye_ref,
                                        sel_ref)
        accf_ref[rows_f, :] = out_f
        accb_ref[rows_b, :] = out_b
        return st_f, m_f, st_b, m_b

    init = []
    for d in range(2):
        if has_state:
            n_rep = jnp.broadcast_to(n0_ref[d], (D_HEAD, D_HEAD))
            init += [jnp.concatenate([c0_ref[d], n_rep], axis=1), m0_ref[d]]
        else:
            init += [jnp.zeros((D_HEAD, 2 * D_HEAD), F32), jnp.zeros((1, D_HEAD), F32)]
    fin = lax.fori_loop(0, n_blocks, scan_step, tuple(init), unroll=min(8, n_blocks))
    if want_state:
        for d in range(2):
            cfin_ref[d] = fin[2 * d][:, :D_HEAD]
            nfin_ref[d] = fin[2 * d][:, D_HEAD:D_HEAD + 1]
            mfin_ref[d] = fin[2 * d + 1]

    norm_w = nw_ref[...]

    def finish(j, _):
        rows = _block_rows(j)
        o = accf_ref[rows, :] + accb_ref[rows, :]
        o = o * lax.rsqrt(jnp.mean(o * o, axis=-1, keepdims=True) + EPS) * norm_w
        o_ref[rows, :] = (o * _sigmoid(og_ref[rows, :])).astype(o_ref.dtype)
        return 0

    lax.fori_loop(0, n_blocks, finish, 0, unroll=min(4, n_blocks))


def _mlstm_mixer(proj, conv_taps, conv_bias, gate_bias, norm_w, states0, batch, seq_len, grid_w,
                 want_state, tables, heads_per_step):
    tri, strict, eye, sel = tables
    has_state = states0 is not None
    n_tok = batch * seq_len

    hp = heads_per_step
    n_groups = N_HEADS // hp

    def col(section):
        return pl.BlockSpec((hp, seq_len, D_HEAD), lambda b, h: (section * n_groups + h, b, 0))

    def const(arr):
        return pl.BlockSpec(arr.shape, lambda b, h: (0,) * arr.ndim)

    def per_head(group_offset):
        return pl.BlockSpec((hp, 1, D_HEAD), lambda b, h: (group_offset + h, 0, 0))

    in_specs = [col(5), col(6), col(7), col(8),
                pl.BlockSpec((None, seq_len, D_HEAD), lambda b, h: (9 * N_HEADS, b, 0)),
                pl.BlockSpec((9, hp, 1, D_HEAD), lambda b, h: (0, h, 0, 0)),
                pl.BlockSpec((9, hp, 1, D_HEAD), lambda b, h: (0, n_groups + h, 0, 0)),
                per_head(0), per_head(n_groups), const(gate_bias), per_head(0),
                const(tri), const(strict), const(eye),
                pl.BlockSpec((hp,) + sel.shape[1:], lambda b, h: (h, 0, 0, 0))]
    args = [proj, proj, proj, proj, proj, conv_taps, conv_taps, conv_bias, conv_bias, gate_bias,
            norm_w, tri, strict, eye, sel]

    def state_spec(rows, lanes):
        return pl.BlockSpec((None, 2, hp, rows, lanes), lambda b, h: (b, 0, h, 0, 0))

    state_specs = [state_spec(D_HEAD, D_HEAD), state_spec(D_HEAD, 1), state_spec(1, D_HEAD)]
    if has_state:
        in_specs += state_specs
        args += list(states0)
    out_shape = [jax.ShapeDtypeStruct((N_HEADS, n_tok, D_HEAD), BF16)]
    out_specs = [pl.BlockSpec((hp, seq_len, D_HEAD), lambda b, h: (h, b, 0))]
    if want_state:
        out_shape += [jax.ShapeDtypeStruct((batch, 2, N_HEADS, D_HEAD, D_HEAD), F32),
                      jax.ShapeDtypeStruct((batch, 2, N_HEADS, D_HEAD, 1), F32),
                      jax.ShapeDtypeStruct((batch, 2, N_HEADS, 1, D_HEAD), F32)]
        out_specs += state_specs
    res = pl.pallas_call(
        functools.partial(_mlstm_kernel, seq_len=seq_len, grid_w=grid_w, heads=hp,
                          has_state=has_state, want_state=want_state),
        grid=(batch, n_groups),
        in_specs=in_specs,
        out_specs=out_specs,
        out_shape=out_shape,
        scratch_shapes=[pltpu.VMEM((seq_len + 2 * grid_w, D_HEAD), F32)] * 2
        + [pltpu.VMEM((seq_len, D_HEAD), F32)] * 2,
        compiler_params=_params("arbitrary", "arbitrary"),
        name="mlstm_mixer",
    )(*args)
    return (res[0], tuple(res[1:])) if want_state else (res[0], None)


def _post_kernel(x_ref, mh_ref, mm_ref, g1_ref, sc2_ref, sh2_ref, g2_ref, n2_ref, woh_ref, wom_ref,
                 wg_ref, wu_ref, wd_ref, fn_ref, o_ref, *, final_norm):
    mix_h = jnp.concatenate([mh_ref[h] for h in range(N_HEADS)], axis=1)
    mix_m = jnp.concatenate([mm_ref[h] for h in range(N_HEADS)], axis=1)
    mixed = _dot(mix_h, woh_ref[...]) + _dot(mix_m, wom_ref[...])
    x1 = x_ref[...] + g1_ref[...] * mixed
    hb = (_rmsnorm(x1, n2_ref[...]) * (1.0 + sc2_ref[...]) + sh2_ref[...]).astype(BF16)
    act = _silu(_dot(hb, wg_ref[...])) * _dot(hb, wu_ref[...])
    x2 = x1 + g2_ref[...] * _dot(act.astype(BF16), wd_ref[...])
    if final_norm:
        x2 = _rmsnorm(x2, fn_ref[...])
    o_ref[...] = x2


def _post_mixer(x, mix_h, mix_m, mod_l, norm2_w, w_out, w_gate, w_up, w_down, final_norm_w, layer,
                row_of_tile, tm, final_norm):
    n_tok = x.shape[0]
    row = lambda i: (i, 0)
    fixed = lambda i: (0, 0)

    def resident(shape, index_map):
        return pl.BlockSpec(shape, index_map, pipeline_mode=pl.Buffered(1))

    return pl.pallas_call(
        functools.partial(_post_kernel, final_norm=final_norm),
        grid=(n_tok // tm,),
        in_specs=[
            pl.BlockSpec((tm, D_MODEL), row),
            pl.BlockSpec((N_HEADS, tm, D_HEAD), lambda i: (0, i, 0)),
            pl.BlockSpec((N_HEADS, tm, D_HEAD), lambda i: (0, i, 0)),
            _mod_spec(2, row_of_tile), _mod_spec(4, row_of_tile), _mod_spec(3, row_of_tile),
            _mod_spec(5, row_of_tile),
            pl.BlockSpec((1, D_MODEL), fixed),
            resident((None, GROUP_WIDTH, D_MODEL), lambda i: (layer, 0, 0)),
            resident((None, GROUP_WIDTH, D_MODEL), lambda i: (layer, 1, 0)),
            resident((None, D_MODEL, D_FF), lambda i: (layer, 0, 0)),
            resident((None, D_MODEL, D_FF), lambda i: (layer, 0, 0)),
            resident((None, D_FF, D_MODEL), lambda i: (layer, 0, 0)),
            pl.BlockSpec((1, D_MODEL), fixed),
        ],
        out_specs=pl.BlockSpec((tm, D_MODEL), row),
        out_shape=jax.ShapeDtypeStruct((n_tok, D_MODEL), F32),
        compiler_params=_params("arbitrary"),
        name="out_projection_ffn",
    )(x, mix_h, mix_m, mod_l, mod_l, mod_l, mod_l, norm2_w, w_out, w_out, w_gate, w_up, w_down,
      final_norm_w)


def _cast_kernel(x_ref, o_ref):
    n_in = x_ref.shape[1]
    n_out = o_ref.shape[1]
    if n_out == n_in:
        o_ref[...] = x_ref[...].astype(o_ref.dtype)
    else:
        n_full = n_in - n_in % D_HEAD
        o_ref[:, :n_full] = x_ref[:, :n_full].astype(o_ref.dtype)
        o_ref[:, n_full:] = jnp.zeros((x_ref.shape[0], n_out - n_full), o_ref.dtype)
        o_ref[:, n_full:n_in] = x_ref[:, n_full:n_in].astype(o_ref.dtype)


def _to_bf16(w, n_out=None):
    depth, rows, cols = w.shape
    n_out = n_out or cols
    tr = int(np.gcd(rows, CAST_ROWS))
    return pl.pallas_call(
        _cast_kernel,
        grid=(depth, rows // tr),
        in_specs=[pl.BlockSpec((None, tr, cols), lambda l, i: (l, i, 0))],
        out_specs=pl.BlockSpec((None, tr, n_out), lambda l, i: (l, i, 0)),
        out_shape=jax.ShapeDtypeStruct((depth, rows, n_out), BF16),
        compiler_params=_params("arbitrary", "arbitrary"),
        name="weights_to_bf16",
    )(w)


def kernel(x_prompt, x_sample, state_hgrn, state_mlstm_c, state_mlstm_n, state_mlstm_m, c, c_ctx,
           norm1_w, norm2_w, w_mod, b_mod, w_in, conv_w, conv_b, ml_gate_b, hg_lb_logits,
           hg_norm_w, ml_norm_w, w_out, w_gate, w_up, w_down, final_norm_w):
    depth = w_in.shape[0]
    n_ctx, ctx_len, _ = x_prompt.shape
    n_lat, lat_len, _ = x_sample.shape
    assert 1 + n_lat <= N_MOD_ROWS and ctx_len % SCAN_BLOCK == 0 and lat_len % SCAN_BLOCK == 0

    hg_tables = _hgrn_tables()
    ml_tables = _mlstm_tables()
    lb_all = _lower_bounds(hg_lb_logits)
    cond = jnp.zeros((N_MOD_ROWS, D_MODEL), F32).at[0].set(c_ctx).at[1:1 + n_lat].set(c)
    mod = _modulation(cond, w_mod, b_mod).reshape(depth, N_MOD_ROWS, 6, 1, D_MODEL)

    tm_in, tm_post = 512, 512
    ctx_row_in = ctx_row_post = lambda i: 0
    lat_row_in = lambda i: 1 + i // (lat_len // tm_in)
    lat_row_post = lambda i: 1 + i // (lat_len // tm_post)

    xp = x_prompt.reshape(n_ctx * ctx_len, D_MODEL)
    xs = x_sample.reshape(n_lat * lat_len, D_MODEL)
    fin_w = final_norm_w.reshape(1, D_MODEL)
    hg_fin, mc_fin, mn_fin, mm_fin = [], [], [], []
    w_in_b = _to_bf16(w_in, N_PROJ_COLS)
    w_out_b, wg_b, wu_b, wd_b = _to_bf16(w_out), _to_bf16(w_gate), _to_bf16(w_up), _to_bf16(w_down)
    for l in range(depth):
        n1 = norm1_w[l].reshape(1, D_MODEL)
        n2 = norm2_w[l].reshape(1, D_MODEL)
        taps = conv_w[l].reshape(9, 2 * N_HEADS, 1, D_HEAD)
        cbias = conv_b[l].reshape(2 * N_HEADS, 1, D_HEAD)
        gbias = jnp.pad(ml_gate_b[l], (0, D_HEAD - N_GATE_COLS)).reshape(1, D_HEAD)
        hg_nw = hg_norm_w[l].reshape(N_HEADS, 1, D_HEAD)
        ml_nw = ml_norm_w[l].reshape(N_HEADS, 1, D_HEAD)
        last = l == depth - 1

        proj = _in_projection(xp, mod[l], n1, w_in_b, l, ctx_row_in, tm_in)
        mix_h, s_h = _hgrn_mixer(proj, lb_all[l], hg_nw, None, n_ctx, ctx_len, True, hg_tables,
                                 CTX_HEADS_PER_STEP)
        mix_m, s_m = _mlstm_mixer(proj, taps, cbias, gbias, ml_nw, None, n_ctx, ctx_len, ctx_len,
                                  True, ml_tables, CTX_HEADS_PER_STEP)
        xp = _post_mixer(xp, mix_h, mix_m, mod[l], n2, w_out_b, wg_b, wu_b, wd_b, fin_w, l,
                         ctx_row_post, tm_post, last)
        hg_fin.append(s_h)
        mc_fin.append(s_m[0])
        mn_fin.append(s_m[1][..., 0])
        mm_fin.append(s_m[2][..., 0, 0])

        cached = (state_mlstm_c[:, l].astype(F32), state_mlstm_n[:, l].astype(F32)[..., None],
                  jnp.broadcast_to(state_mlstm_m[:, l].astype(F32)[..., None, None],
                                   (n_lat, 2, N_HEADS, 1, D_HEAD)))
        proj = _in_projection(xs, mod[l], n1, w_in_b, l, lat_row_in, tm_in)
        mix_h, _ = _hgrn_mixer(proj, lb_all[l], hg_nw, state_hgrn[:, l].astype(F32), n_lat, lat_len,
                               False, hg_tables, LATENT_HEADS_PER_STEP)
        mix_m, _ = _mlstm_mixer(proj, taps, cbias, gbias, ml_nw, cached, n_lat, lat_len,
                                LATENT_GRID_W, False, ml_tables, LATENT_HEADS_PER_STEP)
        xs = _post_mixer(xs, mix_h, mix_m, mod[l], n2, w_out_b, wg_b, wu_b, wd_b, fin_w, l,
                         lat_row_post, tm_post, last)

    return (xp.reshape(x_prompt.shape), xs.reshape(x_sample.shape), jnp.stack(hg_fin, axis=1),
            jnp.stack(mc_fin, axis=1), jnp.stack(mn_fin, axis=1), jnp.stack(mm_fin, axis=1))
```

```python
import functools

import numpy as np
import jax
import jax.numpy as jnp
from jax import lax
from jax.experimental import pallas as pl
from jax.experimental.pallas import tpu as pltpu

F32 = jnp.float32
BF16 = jnp.bfloat16

D_MODEL = 1024
N_HEADS = 4
D_HEAD = 128
GROUP_WIDTH = N_HEADS * D_HEAD
D_FF = 2816
LATENT_GRID_W = 64
EPS = 1e-6
N_MOD_ROWS = 16
N_MAIN_COLS = 9 * GROUP_WIDTH
N_GATE_COLS = 4 * N_HEADS
N_PROJ_COLS = N_MAIN_COLS + D_HEAD
SCAN_BLOCK = 128
HGRN_BLOCK = 128
LEVEL_HALVES = tuple(HGRN_BLOCK >> (i + 1) for i in range(HGRN_BLOCK.bit_length() - 1))
LOG2_E = 1.4426950408889634
CAST_ROWS = 512
TOKEN_TILE = 512
MOD_COLS_TILE = 1536
CTX_HEADS_PER_STEP = 4
LATENT_HEADS_PER_STEP = 1
MLSTM_PREPARED_MAX_BLOCKS = 2
SCAN_UNROLL = 16

VMEM_LIMIT_BYTES = 56 * 1024 * 1024


def _params(*semantics):
    return pltpu.CompilerParams(dimension_semantics=semantics, vmem_limit_bytes=VMEM_LIMIT_BYTES)


def _sigmoid(x):
    return 1.0 / (1.0 + jnp.exp(-x))


def _silu(x):
    return x * _sigmoid(x)


def _log_sigmoid(x):
    return jnp.minimum(x, 0.0) - jnp.log(1.0 + jnp.exp(-jnp.abs(x)))


def _dot(a, b):
    return jnp.dot(a, b, preferred_element_type=F32)


def _dot_nt(a, b):
    return lax.dot_general(a, b, (((1,), (1,)), ((), ())), preferred_element_type=F32)


def _split_bf16(x):
    hi = x.astype(BF16)
    lo = (x - hi.astype(F32)).astype(BF16)
    return jnp.concatenate([hi, lo], axis=1)


def _block_rows(blk, size=SCAN_BLOCK, offset=0, align=None):
    start = blk * size + offset
    return pl.ds(start if isinstance(blk, int) else pl.multiple_of(start, align or size), size)


def _hgrn_tables():
    c = HGRN_BLOCK
    t = np.arange(c)[:, None]
    u = np.arange(c)[None, :]
    scan = np.stack([u <= t, u >= t]).astype(np.float32)
    masks = np.zeros((2, 1 + len(LEVEL_HALVES), c, c), np.float32)
    masks[:, 0] = t == u
    for i, m in enumerate(LEVEL_HALVES):
        upper_t = (t % (2 * m)) >= m
        upper_u = (u % (2 * m)) >= m
        same = (t // (2 * m)) == (u // (2 * m))
        masks[0, 1 + i] = same & upper_t & ~upper_u
        masks[1, 1 + i] = same & ~upper_t & upper_u
    return jnp.asarray(scan, BF16), jnp.asarray(masks, BF16)


def _mlstm_tables():
    c = SCAN_BLOCK
    t = np.arange(c)[:, None]
    u = np.arange(c)[None, :]
    tri = np.stack([u <= t, u >= t]).astype(np.float32)
    strict = np.stack([t > u, t < u]).astype(np.float32)
    sel = np.zeros((N_HEADS, 2 * D_HEAD, 4 * D_HEAD), np.float32)
    for h in range(N_HEADS):
        for d in range(2):
            for part in range(2):
                sel[h, part * D_HEAD + d * N_HEADS + h, 2 * d * D_HEAD:(2 * d + 1) * D_HEAD] = 1.0
                sel[h, part * D_HEAD + (2 + d) * N_HEADS + h,
                    (2 * d + 1) * D_HEAD:(2 * d + 2) * D_HEAD] = 1.0
    return (jnp.asarray(tri, BF16), jnp.asarray(strict, F32),
            jnp.asarray(np.eye(c, dtype=np.float32)), jnp.asarray(sel, BF16))


def _lower_bound_kernel(logit_ref, lb_ref):
    z = logit_ref[...]
    z = z - jnp.max(z, axis=0, keepdims=True)
    e = jnp.exp(z)
    p = e / jnp.sum(e, axis=0, keepdims=True)
    depth = z.shape[0]
    run = p[0:1]
    first = run
    for l in range(depth):
        if l > 0:
            run = run + p[l:l + 1]
        lb_ref[l:l + 1, :] = run - first


def _lower_bounds(hg_lb_logits):
    depth = hg_lb_logits.shape[0]
    flat = hg_lb_logits.astype(F32).reshape(depth, 2 * GROUP_WIDTH)
    lb = pl.pallas_call(
        _lower_bound_kernel,
        out_shape=jax.ShapeDtypeStruct(flat.shape, F32),
        name="hgrn_lower_bounds",
    )(flat)
    return lb.reshape(depth, 2, N_HEADS, 1, D_HEAD)


def _mod_kernel(cond_ref, w_ref, b_ref, o_ref):
    c = cond_ref[...]
    o_ref[0] = _dot(_silu(c).astype(BF16), w_ref[0].astype(BF16)) + b_ref[0]


def _modulation(cond, w_mod, b_mod):
    depth = w_mod.shape[0]
    n_out = w_mod.shape[2]
    tn = MOD_COLS_TILE
    return pl.pallas_call(
        _mod_kernel,
        grid=(depth, n_out // tn),
        in_specs=[
            pl.BlockSpec((N_MOD_ROWS, D_MODEL), lambda l, j: (0, 0)),
            pl.BlockSpec((1, D_MODEL, tn), lambda l, j: (l, 0, j)),
            pl.BlockSpec((1, 1, tn), lambda l, j: (l, 0, j)),
        ],
        out_specs=pl.BlockSpec((1, N_MOD_ROWS, tn), lambda l, j: (l, 0, j)),
        out_shape=jax.ShapeDtypeStruct((depth, N_MOD_ROWS, n_out), F32),
        compiler_params=_params("arbitrary", "arbitrary"),
        name="adaln_modulation",
    )(cond, w_mod, b_mod.reshape(depth, 1, n_out))


def _rmsnorm(x, w):
    return x * lax.rsqrt(jnp.mean(x * x, axis=-1, keepdims=True) + EPS) * w


def _inproj_kernel(x_ref, sc_ref, sh_ref, nw_ref, w_ref, o_ref):
    h = _rmsnorm(x_ref[...], nw_ref[...]) * (1.0 + sc_ref[...]) + sh_ref[...]
    hb = h.astype(BF16)
    for n0 in range(0, N_PROJ_COLS, GROUP_WIDTH):
        n1 = min(n0 + GROUP_WIDTH, N_PROJ_COLS)
        res = _dot(hb, w_ref[:, n0:n1])
        for s in range((n1 - n0) // D_HEAD):
            o_ref[n0 // D_HEAD + s] = res[:, s * D_HEAD:(s + 1) * D_HEAD]


def _mod_spec(which, row_of_tile):
    return pl.BlockSpec((None, None, 1, D_MODEL), lambda i, *_: (row_of_tile(i), which, 0, 0))


def _in_projection(x, mod_l, norm_w, w_in, layer, row_of_tile, tm):
    n_tok = x.shape[0]
    return pl.pallas_call(
        _inproj_kernel,
        grid=(n_tok // tm,),
        in_specs=[
            pl.BlockSpec((tm, D_MODEL), lambda i: (i, 0)),
            _mod_spec(1, row_of_tile),
            _mod_spec(0, row_of_tile),
            pl.BlockSpec((1, D_MODEL), lambda i: (0, 0)),
            pl.BlockSpec((None, D_MODEL, N_PROJ_COLS), lambda i: (layer, 0, 0),
                         pipeline_mode=pl.Buffered(1)),
        ],
        out_specs=pl.BlockSpec((N_PROJ_COLS // D_HEAD, tm, D_HEAD), lambda i: (0, i, 0)),
        out_shape=jax.ShapeDtypeStruct((N_PROJ_COLS // D_HEAD, n_tok, D_HEAD), F32),
        compiler_params=_params("arbitrary"),
        name="norm_in_projection",
    )(x, mod_l, mod_l, norm_w, w_in)


def _hgrn_block(d, q, v, fz, lb, scan_ref, masks_ref, b_scr):
    c = HGRN_BLOCK
    e = jnp.exp(-jnp.abs(fz))
    inv = 1.0 / (1.0 + e)
    pos = fz >= 0.0
    k = (1.0 - lb) * (jnp.where(pos, e, 1.0) * inv)
    f = lb + (1.0 - lb) * (jnp.where(pos, 1.0, e) * inv)
    log2f = jnp.where(lb > 0.0, jnp.log2(f), jnp.minimum(fz, 0.0) * LOG2_E - jnp.log2(1.0 + e))

    b2 = _dot(scan_ref[d], _split_bf16(log2f))
    b = b2[:, :D_HEAD] + b2[:, D_HEAD:]
    total_row = c - 1 if d == 0 else 0
    total = b[total_row:total_row + 1]
    decay_in = jnp.exp2(b)
    decay_out = jnp.exp2(total - b)
    vb = v.astype(BF16)

    b_scr[...] = b

    scores = _dot_nt(q.astype(BF16), k.astype(BF16)).astype(BF16) * masks_ref[d, 0]
    t_idx = lax.broadcasted_iota(jnp.int32, (c, D_HEAD), 0)
    f_prev = pltpu.roll(f, 1, 0)
    f_next = pltpu.roll(f, c - 1, 0)
    for i, m in enumerate(LEVEL_HALVES):
        if m >= 8:
            parts = []
            for g in range(c // (2 * m)):
                ref_row = b_scr[g * 2 * m + (m - 1 if d == 0 else m):
                                g * 2 * m + (m - 1 if d == 0 else m) + 1, :]
                low = slice(g * 2 * m, g * 2 * m + m)
                high = slice(g * 2 * m + m, (g + 1) * 2 * m)
                if d == 0:
                    parts += [k[low] * jnp.exp2(ref_row - b[low]), q[high] * jnp.exp2(b[high] - ref_row)]
                else:
                    parts += [q[low] * jnp.exp2(b[low] - ref_row), k[high] * jnp.exp2(ref_row - b[high])]
            z = jnp.concatenate(parts, axis=0).astype(BF16)
            scores = scores + _dot_nt(z, z).astype(BF16) * masks_ref[d, 1 + i]
            continue
        if m == 4:
            parts = []
            for g in range(c // (2 * m)):
                r = g * 2 * m + (m - 1 if d == 0 else m)
                parts.append(jnp.exp2(-jnp.abs(b[g * 2 * m:(g + 1) * 2 * m] - b_scr[r:r + 1, :])))
            e_lvl = jnp.concatenate(parts, axis=0)
        elif m == 2:
            ph = t_idx % 4
            if d == 0:
                e_lvl = jnp.where(ph == 0, f_next, jnp.where(ph == 1, 1.0,
                                  jnp.where(ph == 2, f, f * f_prev)))
            else:
                e_lvl = jnp.where(ph == 0, f * f_next, jnp.where(ph == 1, f,
                                  jnp.where(ph == 2, 1.0, f_prev)))
        else:
            e_lvl = jnp.where(t_idx % 2 == (1 - d), f, 1.0)
        s_lvl = _dot_nt((q * e_lvl).astype(BF16), (k * e_lvl).astype(BF16))
        scores = scores + s_lvl.astype(BF16) * masks_ref[d, 1 + i]

    intra = _dot(scores, vb)
    q_in = (q * decay_in).astype(BF16)
    kv_t = _dot(v.T.astype(BF16), (k * decay_out).astype(BF16))
    return intra, q_in, kv_t, jnp.exp2(total)


def _hgrn_kernel(*refs, seq_len, heads, has_state, want_state):
    q_ref, ff_ref, fb_ref, v_ref, g_ref, lb_ref, nw_ref, scan_ref, masks_ref = refs[:9]
    pos = 9
    s0_ref = None
    if has_state:
        s0_ref = refs[pos]
        pos += 1
    o_ref = refs[pos]
    pos += 1
    sfin_ref = None
    if want_state:
        sfin_ref = refs[pos]
        pos += 1
    acc_ref, b_scr = refs[pos:pos + 2]
    n_blocks = seq_len // HGRN_BLOCK
    fz_refs = (ff_ref, fb_ref)
    chains = [(h, d) for h in range(heads) for d in range(2)]

    def scan_step(j, carry):
        states = list(carry)
        for i, (h, d) in enumerate(chains):
            rows = _block_rows(j if d == 0 else n_blocks - 1 - j, HGRN_BLOCK)
            intra, q_in, kv_t, decay = _hgrn_block(d, _silu(q_ref[h, rows, :]), v_ref[h, rows, :],
                                                   fz_refs[d][h, rows, :], lb_ref[d, h], scan_ref,
                                                   masks_ref, b_scr.at[h, d])
            acc_ref[h, d, rows, :] = intra + _dot_nt(q_in, states[i].astype(BF16))
            states[i] = states[i] * decay + kv_t
        return tuple(states)

    if has_state:
        init = tuple(s0_ref[d, h].T for h, d in chains)
    else:
        init = (jnp.zeros((D_HEAD, D_HEAD), F32),) * len(chains)
    final = lax.fori_loop(0, n_blocks, scan_step, init,
                          unroll=min(n_blocks, max(2, SCAN_UNROLL // heads)))
    if want_state:
        for i, (h, d) in enumerate(chains):
            sfin_ref[d, h] = final[i].T

    def finish(j, _):
        rows = _block_rows(j, HGRN_BLOCK)
        for h in range(heads):
            o = acc_ref[h, 0, rows, :] + acc_ref[h, 1, rows, :]
            o = o * lax.rsqrt(jnp.mean(o * o, axis=-1, keepdims=True) + EPS) * nw_ref[h]
            o_ref[h, rows, :] = (o * _silu(g_ref[h, rows, :])).astype(o_ref.dtype)
        return 0

    lax.fori_loop(0, n_blocks, finish, 0, unroll=min(n_blocks, max(2, 4 // heads)))


def _hgrn_mixer(proj, lb_l, norm_w, state0, batch, seq_len, want_state, tables, heads_per_step):
    scan, masks = tables
    has_state = state0 is not None
    n_tok = batch * seq_len

    hp = heads_per_step
    n_groups = N_HEADS // hp

    def col(section):
        return pl.BlockSpec((hp, seq_len, D_HEAD), lambda b, h: (section * n_groups + h, b, 0))

    in_specs = [col(0), col(1), col(2), col(3), col(4),
                pl.BlockSpec((2, hp, 1, D_HEAD), lambda b, h: (0, h, 0, 0)),
                pl.BlockSpec((hp, 1, D_HEAD), lambda b, h: (h, 0, 0)),
                pl.BlockSpec(scan.shape, lambda b, h: (0, 0, 0)),
                pl.BlockSpec(masks.shape, lambda b, h: (0, 0, 0, 0))]
    args = [proj, proj, proj, proj, proj, lb_l, norm_w, scan, masks]
    state_spec = pl.BlockSpec((None, 2, hp, D_HEAD, D_HEAD), lambda b, h: (b, 0, h, 0, 0))
    if has_state:
        in_specs.append(state_spec)
        args.append(state0)
    out_shape = [jax.ShapeDtypeStruct((N_HEADS, n_tok, D_HEAD), BF16)]
    out_specs = [pl.BlockSpec((hp, seq_len, D_HEAD), lambda b, h: (h, b, 0))]
    if want_state:
        out_shape.append(jax.ShapeDtypeStruct((batch, 2, N_HEADS, D_HEAD, D_HEAD), F32))
        out_specs.append(state_spec)
    res = pl.pallas_call(
        functools.partial(_hgrn_kernel, seq_len=seq_len, heads=hp, has_state=has_state,
                          want_state=want_state),
        grid=(batch, n_groups),
        in_specs=in_specs,
        out_specs=out_specs,
        out_shape=out_shape,
        scratch_shapes=[pltpu.VMEM((hp, 2, seq_len, D_HEAD), F32),
                        pltpu.VMEM((hp, 2, HGRN_BLOCK, D_HEAD), F32)],
        compiler_params=_params("arbitrary", "arbitrary"),
        name="hgrn2_mixer",
    )(*args)
    return res if want_state else (res[0], None)


def _dwconv_rows(x_ref, taps_ref, bias_ref, y_ref, seq_len, grid_w):
    w = grid_w
    n_rows = seq_len // w
    taps = taps_ref[...]
    bias = jnp.broadcast_to(bias_ref[...], (w, x_ref.shape[1]))
    col = lax.broadcasted_iota(jnp.int32, (w, x_ref.shape[1]), 0)
    y_ref[pl.ds(0, w), :] = jnp.zeros_like(bias)
    y_ref[pl.ds(w, w), :] = bias

    def grid_row(r, _):
        base = pl.multiple_of(r * w, w)
        x = x_ref[pl.ds(base, w), :]
        left = jnp.where(col == 0, 0.0, pltpu.roll(x, 1, 0))
        right = jnp.where(col == w - 1, 0.0, pltpu.roll(x, w - 1, 0))

        def kernel_row(i):
            return left * taps[3 * i] + x * taps[3 * i + 1] + right * taps[3 * i + 2]

        y_ref[pl.ds(base + 2 * w, w), :] = kernel_row(0) + bias
        y_ref[pl.ds(base + w, w), :] += kernel_row(1)
        y_ref[pl.ds(base, w), :] += kernel_row(2)
        return 0

    lax.fori_loop(0, n_rows, grid_row, 0)


def _mlstm_gates(gates, sel):
    spread = _dot(_split_bf16(gates), sel)
    return [spread[:, i * D_HEAD:(i + 1) * D_HEAD] for i in range(sel.shape[1] // D_HEAD)]


def _mlstm_block(d, q, k, v, log_i, log_f, state, m_run, tri_ref, strict_ref, eye_ref):
    c = SCAN_BLOCK

    tri = tri_ref[d]
    vis = (strict_ref[d] + eye_ref[...]) > 0.5
    mix = log_f * strict_ref[d] + log_i * eye_ref[...]
    d2 = _dot(tri, _split_bf16(mix))
    d_log = d2[:, :c] + d2[:, c:]
    b2 = _dot(tri, _split_bf16(log_f))
    b_run = b2[:, :D_HEAD] + b2[:, D_HEAD:]

    inter_log = b_run + m_run
    d_max = jnp.max(jnp.where(vis, d_log, -jnp.inf), axis=1, keepdims=True)
    m_t = jnp.maximum(inter_log, jnp.broadcast_to(d_max, (c, D_HEAD)))
    p = jnp.where(vis, jnp.exp(d_log - m_t), 0.0)
    scores = _dot_nt(q.astype(BF16), k.astype(BF16)) * p
    v_aug = jnp.concatenate([v, jnp.ones((c, D_HEAD), F32)], axis=1).astype(BF16)
    intra = _dot(scores.astype(BF16), v_aug)
    inter = _dot(q.astype(BF16), state.astype(BF16))
    a_inter = jnp.exp(inter_log - m_t)
    num = intra[:, :D_HEAD] + a_inter * inter[:, :D_HEAD]
    den = intra[:, D_HEAD:] + a_inter * inter[:, D_HEAD:]
    out = num / jnp.maximum(jnp.abs(den), jnp.exp(-m_t))

    last_row = c - 1 if d == 0 else 0
    b_last = b_run[last_row:last_row + 1]
    w_end = b_last - b_run + log_i
    m_loc = jnp.max(w_end, axis=0, keepdims=True)
    p_end = jnp.exp(w_end - m_loc)
    kv = _dot((k * p_end).T.astype(BF16), v_aug)
    m_new = jnp.maximum(b_last + m_run, m_loc)
    a = jnp.exp(b_last + m_run - m_new)
    g = jnp.exp(m_loc - m_new)
    new_state = (jnp.concatenate([a, a], axis=1) * state + jnp.concatenate([g, g], axis=1) * kv)
    return out, new_state, m_new


def _mlstm_kernel(*refs, seq_len, grid_w, heads, has_state, want_state):
    (q_ref, k_ref, v_ref, og_ref, gates_ref, qtap_ref, ktap_ref, qb_ref, kb_ref, gb_ref, nw_ref,
     tri_ref, strict_ref, eye_ref, sel_ref) = refs[:15]
    pos = 15
    state_in = state_out = (None, None, None)
    if has_state:
        state_in = refs[pos:pos + 3]
        pos += 3
    o_ref = refs[pos]
    pos += 1
    if want_state:
        state_out = refs[pos:pos + 3]
        pos += 3
    scratch = refs[pos:pos + 5]
    for h in range(heads):
        _mlstm_head(q_ref.at[h], k_ref.at[h], v_ref.at[h], og_ref.at[h], gates_ref, qtap_ref.at[:, h],
                    ktap_ref.at[:, h], qb_ref.at[h], kb_ref.at[h], gb_ref, nw_ref.at[h], tri_ref,
                    strict_ref, eye_ref, sel_ref.at[h],
                    tuple(r.at[:, h] for r in state_in) if has_state else state_in,
                    o_ref.at[h],
                    tuple(r.at[:, h] for r in state_out) if want_state else state_out,
                    scratch, seq_len, grid_w)


def _mlstm_head(q_ref, k_ref, v_ref, og_ref, gates_ref, qtap_ref, ktap_ref, qb_ref, kb_ref, gb_ref,
                nw_ref, tri_ref, strict_ref, eye_ref, sel_ref, state_in, o_ref, state_out, scratch,
                seq_len, grid_w):
    c0_ref, n0_ref, m0_ref = state_in
    cfin_ref, nfin_ref, mfin_ref = state_out
    has_state = c0_ref is not None
    want_state = cfin_ref is not None
    yq_ref, yk_ref, accf_ref, accb_ref, gsp_ref = scratch

    n_blocks = seq_len // SCAN_BLOCK
    _dwconv_rows(q_ref, qtap_ref, qb_ref, yq_ref, seq_len, grid_w)
    _dwconv_rows(k_ref, ktap_ref, kb_ref, yk_ref, seq_len, grid_w)

    gate_bias = gb_ref[...]
    conv_align = int(np.gcd(grid_w, SCAN_BLOCK))

    def conv_rows(blk):
        return _block_rows(blk, SCAN_BLOCK, grid_w, conv_align)

    def activations(blk):
        cr = conv_rows(blk)
        return _silu(yq_ref[cr, :]), _silu(yk_ref[cr, :]) * (D_HEAD ** -0.5)

    prepared = n_blocks <= MLSTM_PREPARED_MAX_BLOCKS
    if prepared:
        for blk in range(n_blocks):
            rows, cr = _block_rows(blk), conv_rows(blk)
            yq_ref[cr, :], yk_ref[cr, :] = activations(blk)
            li_f, fz_f, li_b, fz_b = _mlstm_gates(gates_ref[rows, :] + gate_bias, sel_ref[...])
            for i, val in enumerate((li_f, _log_sigmoid(fz_f), li_b, _log_sigmoid(fz_b))):
                gsp_ref[rows, i * D_HEAD:(i + 1) * D_HEAD] = val

    def block_inputs(d, blk):
        rows, cr = _block_rows(blk), conv_rows(blk)
        if prepared:
            q, k = yq_ref[cr, :], yk_ref[cr, :]
            log_i = gsp_ref[rows, 2 * d * D_HEAD:(2 * d + 1) * D_HEAD]
            log_f = gsp_ref[rows, (2 * d + 1) * D_HEAD:(2 * d + 2) * D_HEAD]
        else:
            q, k = activations(blk)
            log_i, fz = _mlstm_gates(gates_ref[rows, :] + gate_bias,
                                     sel_ref[:, 2 * d * D_HEAD:(2 * d + 2) * D_HEAD])
            log_f = _log_sigmoid(fz)
        return q, k, v_ref[rows, :], log_i, log_f

    def scan_step(j, carry):
        st_f, m_f, st_b, m_b = carry
        blk_b = n_blocks - 1 - j
        out_f, st_f, m_f = _mlstm_block(0, *block_inputs(0, j), st_f, m_f, tri_ref, strict_ref,
                                        eye_ref)
        out_b, st_b, m_b = _mlstm_block(1, *block_inputs(1, blk_b), st_b, m_b, tri_ref, strict_ref,
                                        eye_ref)
        accf_ref[_block_rows(j), :] = out_f
        accb_ref[_block_rows(blk_b), :] = out_b
        return st_f, m_f, st_b, m_b

    init = []
    for d in range(2):
        if has_state:
            n_rep = jnp.broadcast_to(n0_ref[d], (D_HEAD, D_HEAD))
            init += [jnp.concatenate([c0_ref[d], n_rep], axis=1), m0_ref[d]]
        else:
            init += [jnp.zeros((D_HEAD, 2 * D_HEAD), F32), jnp.zeros((1, D_HEAD), F32)]
    fin = lax.fori_loop(0, n_blocks, scan_step, tuple(init), unroll=min(16, n_blocks))
    if want_state:
        for d in range(2):
            cfin_ref[d] = fin[2 * d][:, :D_HEAD]
            nfin_ref[d] = fin[2 * d][:, D_HEAD:D_HEAD + 1]
            mfin_ref[d] = fin[2 * d + 1]

    norm_w = nw_ref[...]

    def finish(j, _):
        rows = _block_rows(j)
        o = accf_ref[rows, :] + accb_ref[rows, :]
        o = o * lax.rsqrt(jnp.mean(o * o, axis=-1, keepdims=True) + EPS) * norm_w
        o_ref[rows, :] = (o * _sigmoid(og_ref[rows, :])).astype(o_ref.dtype)
        return 0

    lax.fori_loop(0, n_blocks, finish, 0, unroll=min(4, n_blocks))


def _mlstm_mixer(proj, conv_taps, conv_bias, gate_bias, norm_w, states0, batch, seq_len, grid_w,
                 want_state, tables, heads_per_step):
    tri, strict, eye, sel = tables
    has_state = states0 is not None
    n_tok = batch * seq_len

    hp = heads_per_step
    n_groups = N_HEADS // hp

    def col(section):
        return pl.BlockSpec((hp, seq_len, D_HEAD), lambda b, h: (section * n_groups + h, b, 0))

    def const(arr):
        return pl.BlockSpec(arr.shape, lambda b, h: (0,) * arr.ndim)

    def per_head(group_offset):
        return pl.BlockSpec((hp, 1, D_HEAD), lambda b, h: (group_offset + h, 0, 0))

    in_specs = [col(5), col(6), col(7), col(8),
                pl.BlockSpec((None, seq_len, D_HEAD), lambda b, h: (9 * N_HEADS, b, 0)),
                pl.BlockSpec((9, hp, 1, D_HEAD), lambda b, h: (0, h, 0, 0)),
                pl.BlockSpec((9, hp, 1, D_HEAD), lambda b, h: (0, n_groups + h, 0, 0)),
                per_head(0), per_head(n_groups), const(gate_bias), per_head(0),
                const(tri), const(strict), const(eye),
                pl.BlockSpec((hp,) + sel.shape[1:], lambda b, h: (h, 0, 0))]
    args = [proj, proj, proj, proj, proj, conv_taps, conv_taps, conv_bias, conv_bias, gate_bias,
            norm_w, tri, strict, eye, sel]

    def state_spec(rows, lanes):
        return pl.BlockSpec((None, 2, hp, rows, lanes), lambda b, h: (b, 0, h, 0, 0))

    state_specs = [state_spec(D_HEAD, D_HEAD), state_spec(D_HEAD, 1), state_spec(1, D_HEAD)]
    if has_state:
        in_specs += state_specs
        args += list(states0)
    out_shape = [jax.ShapeDtypeStruct((N_HEADS, n_tok, D_HEAD), BF16)]
    out_specs = [pl.BlockSpec((hp, seq_len, D_HEAD), lambda b, h: (h, b, 0))]
    if want_state:
        out_shape += [jax.ShapeDtypeStruct((batch, 2, N_HEADS, D_HEAD, D_HEAD), F32),
                      jax.ShapeDtypeStruct((batch, 2, N_HEADS, D_HEAD, 1), F32),
                      jax.ShapeDtypeStruct((batch, 2, N_HEADS, 1, D_HEAD), F32)]
        out_specs += state_specs
    res = pl.pallas_call(
        functools.partial(_mlstm_kernel, seq_len=seq_len, grid_w=grid_w, heads=hp,
                          has_state=has_state, want_state=want_state),
        grid=(batch, n_groups),
        in_specs=in_specs,
        out_specs=out_specs,
        out_shape=out_shape,
        scratch_shapes=[pltpu.VMEM((seq_len + 2 * grid_w, D_HEAD), F32)] * 2
        + [pltpu.VMEM((seq_len, D_HEAD), F32)] * 2
        + [pltpu.VMEM((seq_len if seq_len // SCAN_BLOCK <= MLSTM_PREPARED_MAX_BLOCKS
                       else SCAN_BLOCK, 4 * D_HEAD), F32)],
        compiler_params=_params("arbitrary", "arbitrary"),
        name="mlstm_mixer",
    )(*args)
    return (res[0], tuple(res[1:])) if want_state else (res[0], None)


def _post_kernel(x_ref, mh_ref, mm_ref, g1_ref, sc2_ref, sh2_ref, g2_ref, n2_ref, woh_ref, wom_ref,
                 wg_ref, wu_ref, wd_ref, fn_ref, o_ref, *, final_norm):
    mix_h = jnp.concatenate([mh_ref[h] for h in range(N_HEADS)], axis=1)
    mix_m = jnp.concatenate([mm_ref[h] for h in range(N_HEADS)], axis=1)
    mixed = _dot(mix_h, woh_ref[...]) + _dot(mix_m, wom_ref[...])
    x1 = x_ref[...] + g1_ref[...] * mixed
    hb = (_rmsnorm(x1, n2_ref[...]) * (1.0 + sc2_ref[...]) + sh2_ref[...]).astype(BF16)
    act = _silu(_dot(hb, wg_ref[...])) * _dot(hb, wu_ref[...])
    x2 = x1 + g2_ref[...] * _dot(act.astype(BF16), wd_ref[...])
    if final_norm:
        x2 = _rmsnorm(x2, fn_ref[...])
    o_ref[...] = x2


def _post_mixer(x, mix_h, mix_m, mod_l, norm2_w, w_out, w_gate, w_up, w_down, final_norm_w, layer,
                row_of_tile, tm, final_norm):
    n_tok = x.shape[0]
    row = lambda i: (i, 0)
    fixed = lambda i: (0, 0)

    def resident(shape, index_map):
        return pl.BlockSpec(shape, index_map, pipeline_mode=pl.Buffered(1))

    return pl.pallas_call(
        functools.partial(_post_kernel, final_norm=final_norm),
        grid=(n_tok // tm,),
        in_specs=[
            pl.BlockSpec((tm, D_MODEL), row),
            pl.BlockSpec((N_HEADS, tm, D_HEAD), lambda i: (0, i, 0)),
            pl.BlockSpec((N_HEADS, tm, D_HEAD), lambda i: (0, i, 0)),
            _mod_spec(2, row_of_tile), _mod_spec(4, row_of_tile), _mod_spec(3, row_of_tile),
            _mod_spec(5, row_of_tile),
            pl.BlockSpec((1, D_MODEL), fixed),
            resident((None, GROUP_WIDTH, D_MODEL), lambda i: (layer, 0, 0)),
            resident((None, GROUP_WIDTH, D_MODEL), lambda i: (layer, 1, 0)),
            resident((None, D_MODEL, D_FF), lambda i: (layer, 0, 0)),
            resident((None, D_MODEL, D_FF), lambda i: (layer, 0, 0)),
            resident((None, D_FF, D_MODEL), lambda i: (layer, 0, 0)),
            pl.BlockSpec((1, D_MODEL), fixed),
        ],
        out_specs=pl.BlockSpec((tm, D_MODEL), row),
        out_shape=jax.ShapeDtypeStruct((n_tok, D_MODEL), F32),
        compiler_params=_params("arbitrary"),
        name="out_projection_ffn",
    )(x, mix_h, mix_m, mod_l, mod_l, mod_l, mod_l, norm2_w, w_out, w_out, w_gate, w_up, w_down,
      final_norm_w)


def _cast_kernel(x_ref, o_ref):
    n_in = x_ref.shape[1]
    n_out = o_ref.shape[1]
    if n_out == n_in:
        o_ref[...] = x_ref[...].astype(o_ref.dtype)
    else:
        n_full = n_in - n_in % D_HEAD
        o_ref[:, :n_full] = x_ref[:, :n_full].astype(o_ref.dtype)
        o_ref[:, n_full:] = jnp.zeros((x_ref.shape[0], n_out - n_full), o_ref.dtype)
        o_ref[:, n_full:n_in] = x_ref[:, n_full:n_in].astype(o_ref.dtype)


def _to_bf16(w, n_out=None):
    depth, rows, cols = w.shape
    n_out = n_out or cols
    tr = int(np.gcd(rows, CAST_ROWS))
    return pl.pallas_call(
        _cast_kernel,
        grid=(depth, rows // tr),
        in_specs=[pl.BlockSpec((None, tr, cols), lambda l, i: (l, i, 0))],
        out_specs=pl.BlockSpec((None, tr, n_out), lambda l, i: (l, i, 0)),
        out_shape=jax.ShapeDtypeStruct((depth, rows, n_out), BF16),
        compiler_params=_params("arbitrary", "arbitrary"),
        name="weights_to_bf16",
    )(w)


def kernel(x_prompt, x_sample, state_hgrn, state_mlstm_c, state_mlstm_n, state_mlstm_m, c, c_ctx,
           norm1_w, norm2_w, w_mod, b_mod, w_in, conv_w, conv_b, ml_gate_b, hg_lb_logits,
           hg_norm_w, ml_norm_w, w_out, w_gate, w_up, w_down, final_norm_w):
    depth = w_in.shape[0]
    n_ctx, ctx_len, _ = x_prompt.shape
    n_lat, lat_len, _ = x_sample.shape
    assert 1 + n_lat <= N_MOD_ROWS and ctx_len % SCAN_BLOCK == 0 and lat_len % SCAN_BLOCK == 0

    hg_tables = _hgrn_tables()
    ml_tables = _mlstm_tables()
    lb_all = _lower_bounds(hg_lb_logits)
    cond = jnp.zeros((N_MOD_ROWS, D_MODEL), F32).at[0].set(c_ctx).at[1:1 + n_lat].set(c)
    mod = _modulation(cond, w_mod, b_mod).reshape(depth, N_MOD_ROWS, 6, 1, D_MODEL)

    tm_in = tm_post = TOKEN_TILE
    ctx_row_in = ctx_row_post = lambda i: 0
    lat_row_in = lambda i: 1 + i // (lat_len // tm_in)
    lat_row_post = lambda i: 1 + i // (lat_len // tm_post)

    xp = x_prompt.reshape(n_ctx * ctx_len, D_MODEL)
    xs = x_sample.reshape(n_lat * lat_len, D_MODEL)
    fin_w = final_norm_w.reshape(1, D_MODEL)
    hg_fin, mc_fin, mn_fin, mm_fin = [], [], [], []
    w_in_b = _to_bf16(w_in, N_PROJ_COLS)
    w_out_b, wg_b, wu_b, wd_b = _to_bf16(w_out), _to_bf16(w_gate), _to_bf16(w_up), _to_bf16(w_down)
    for l in range(depth):
        n1 = norm1_w[l].reshape(1, D_MODEL)
        n2 = norm2_w[l].reshape(1, D_MODEL)
        taps = conv_w[l].reshape(9, 2 * N_HEADS, 1, D_HEAD)
        cbias = conv_b[l].reshape(2 * N_HEADS, 1, D_HEAD)
        gbias = jnp.pad(ml_gate_b[l], (0, D_HEAD - N_GATE_COLS)).reshape(1, D_HEAD)
        hg_nw = hg_norm_w[l].reshape(N_HEADS, 1, D_HEAD)
        ml_nw = ml_norm_w[l].reshape(N_HEADS, 1, D_HEAD)
        last = l == depth - 1

        proj = _in_projection(xp, mod[l], n1, w_in_b, l, ctx_row_in, tm_in)
        mix_h, s_h = _hgrn_mixer(proj, lb_all[l], hg_nw, None, n_ctx, ctx_len, True, hg_tables,
                                 CTX_HEADS_PER_STEP)
        mix_m, s_m = _mlstm_mixer(proj, taps, cbias, gbias, ml_nw, None, n_ctx, ctx_len, ctx_len,
                                  True, ml_tables, CTX_HEADS_PER_STEP)
        xp = _post_mixer(xp, mix_h, mix_m, mod[l], n2, w_out_b, wg_b, wu_b, wd_b, fin_w, l,
                         ctx_row_post, tm_post, last)
        hg_fin.append(s_h)
        mc_fin.append(s_m[0])
        mn_fin.append(s_m[1][..., 0])
        mm_fin.append(s_m[2][..., 0, 0])

        cached = (state_mlstm_c[:, l].astype(F32), state_mlstm_n[:, l].astype(F32)[..., None],
                  jnp.broadcast_to(state_mlstm_m[:, l].astype(F32)[..., None, None],
                                   (n_lat, 2, N_HEADS, 1, D_HEAD)))
        proj = _in_projection(xs, mod[l], n1, w_in_b, l, lat_row_in, tm_in)
        mix_h, _ = _hgrn_mixer(proj, lb_all[l], hg_nw, state_hgrn[:, l].astype(F32), n_lat, lat_len,
                               False, hg_tables, LATENT_HEADS_PER_STEP)
        mix_m, _ = _mlstm_mixer(proj, taps, cbias, gbias, ml_nw, cached, n_lat, lat_len,
                                LATENT_GRID_W, False, ml_tables, LATENT_HEADS_PER_STEP)
        xs = _post_mixer(xs, mix_h, mix_m, mod[l], n2, w_out_b, wg_b, wu_b, wd_b, fin_w, l,
                         lat_row_post, tm_post, last)

    return (xp.reshape(x_prompt.shape), xs.reshape(x_sample.shape), jnp.stack(hg_fin, axis=1),
            jnp.stack(mc_fin, axis=1), jnp.stack(mn_fin, axis=1), jnp.stack(mm_fin, axis=1))
```

```python
import functools

import numpy as np
import jax
import jax.numpy as jnp
from jax import lax
from jax.experimental import pallas as pl
from jax.experimental.pallas import tpu as pltpu

F32 = jnp.float32
BF16 = jnp.bfloat16

D_MODEL = 1024
N_HEADS = 4
D_HEAD = 128
GROUP_WIDTH = N_HEADS * D_HEAD
D_FF = 2816
LATENT_GRID_W = 64
EPS = 1e-6
N_MOD_ROWS = 16
N_MAIN_COLS = 9 * GROUP_WIDTH
N_GATE_COLS = 4 * N_HEADS
N_PROJ_COLS = N_MAIN_COLS + D_HEAD
SCAN_BLOCK = 128
HGRN_BLOCK = 128
LEVEL_HALVES = tuple(HGRN_BLOCK >> (i + 1) for i in range(HGRN_BLOCK.bit_length() - 1))
LOG2_E = 1.4426950408889634
CAST_ROWS = 512
TOKEN_TILE = 512
MOD_COLS_TILE = 1536
CTX_HEADS_PER_STEP = 4
LATENT_HEADS_PER_STEP = 1
MLSTM_PREPARED_MAX_BLOCKS = 2
SCAN_UNROLL = 16

VMEM_LIMIT_BYTES = 56 * 1024 * 1024


def _params(*semantics):
    return pltpu.CompilerParams(dimension_semantics=semantics, vmem_limit_bytes=VMEM_LIMIT_BYTES)


def _sigmoid(x):
    return 1.0 / (1.0 + jnp.exp(-x))


def _silu(x):
    return x * _sigmoid(x)


def _log_sigmoid(x):
    return jnp.minimum(x, 0.0) - jnp.log(1.0 + jnp.exp(-jnp.abs(x)))


def _dot(a, b):
    return jnp.dot(a, b, preferred_element_type=F32)


def _dot_nt(a, b):
    return lax.dot_general(a, b, (((1,), (1,)), ((), ())), preferred_element_type=F32)


def _split_bf16(x):
    hi = x.astype(BF16)
    lo = (x - hi.astype(F32)).astype(BF16)
    return jnp.concatenate([hi, lo], axis=1)


def _block_rows(blk, size=SCAN_BLOCK, offset=0, align=None):
    start = blk * size + offset
    return pl.ds(start if isinstance(blk, int) else pl.multiple_of(start, align or size), size)


def _hgrn_tables():
    c = HGRN_BLOCK
    t = np.arange(c)[:, None]
    u = np.arange(c)[None, :]
    scan = np.stack([u <= t, u >= t]).astype(np.float32)
    masks = np.zeros((2, 1 + len(LEVEL_HALVES), c, c), np.float32)
    masks[:, 0] = t == u
    for i, m in enumerate(LEVEL_HALVES):
        upper_t = (t % (2 * m)) >= m
        upper_u = (u % (2 * m)) >= m
        same = (t // (2 * m)) == (u // (2 * m))
        masks[0, 1 + i] = same & upper_t & ~upper_u
        masks[1, 1 + i] = same & ~upper_t & upper_u
    return jnp.asarray(scan, BF16), jnp.asarray(masks, BF16)


def _mlstm_tables():
    c = SCAN_BLOCK
    t = np.arange(c)[:, None]
    u = np.arange(c)[None, :]
    tri = np.stack([u <= t, u >= t]).astype(np.float32)
    strict = np.stack([t > u, t < u]).astype(np.float32)
    sel = np.zeros((N_HEADS, 2 * D_HEAD, 4 * D_HEAD), np.float32)
    for h in range(N_HEADS):
        for d in range(2):
            for part in range(2):
                sel[h, part * D_HEAD + d * N_HEADS + h, 2 * d * D_HEAD:(2 * d + 1) * D_HEAD] = 1.0
                sel[h, part * D_HEAD + (2 + d) * N_HEADS + h,
                    (2 * d + 1) * D_HEAD:(2 * d + 2) * D_HEAD] = 1.0
    return (jnp.asarray(tri, BF16), jnp.asarray(strict, F32),
            jnp.asarray(np.eye(c, dtype=np.float32)), jnp.asarray(sel, BF16))


def _lower_bound_kernel(logit_ref, lb_ref):
    z = logit_ref[...]
    z = z - jnp.max(z, axis=0, keepdims=True)
    e = jnp.exp(z)
    p = e / jnp.sum(e, axis=0, keepdims=True)
    depth = z.shape[0]
    run = p[0:1]
    first = run
    for l in range(depth):
        if l > 0:
            run = run + p[l:l + 1]
        lb_ref[l:l + 1, :] = run - first


def _lower_bounds(hg_lb_logits):
    depth = hg_lb_logits.shape[0]
    flat = hg_lb_logits.astype(F32).reshape(depth, 2 * GROUP_WIDTH)
    lb = pl.pallas_call(
        _lower_bound_kernel,
        out_shape=jax.ShapeDtypeStruct(flat.shape, F32),
        name="hgrn_lower_bounds",
    )(flat)
    return lb.reshape(depth, 2, N_HEADS, 1, D_HEAD)


def _mod_kernel(cond_ref, w_ref, b_ref, o_ref):
    c = cond_ref[...]
    o_ref[0] = _dot(_silu(c).astype(BF16), w_ref[0].astype(BF16)) + b_ref[0]


def _modulation(cond, w_mod, b_mod):
    depth = w_mod.shape[0]
    n_out = w_mod.shape[2]
    tn = MOD_COLS_TILE
    return pl.pallas_call(
        _mod_kernel,
        grid=(depth, n_out // tn),
        in_specs=[
            pl.BlockSpec((N_MOD_ROWS, D_MODEL), lambda l, j: (0, 0)),
            pl.BlockSpec((1, D_MODEL, tn), lambda l, j: (l, 0, j)),
            pl.BlockSpec((1, 1, tn), lambda l, j: (l, 0, j)),
        ],
        out_specs=pl.BlockSpec((1, N_MOD_ROWS, tn), lambda l, j: (l, 0, j)),
        out_shape=jax.ShapeDtypeStruct((depth, N_MOD_ROWS, n_out), F32),
        compiler_params=_params("arbitrary", "arbitrary"),
        name="adaln_modulation",
    )(cond, w_mod, b_mod.reshape(depth, 1, n_out))


def _rmsnorm(x, w):
    return x * lax.rsqrt(jnp.mean(x * x, axis=-1, keepdims=True) + EPS) * w


def _inproj_kernel(x_ref, sc_ref, sh_ref, nw_ref, w_ref, o_ref):
    h = _rmsnorm(x_ref[...], nw_ref[...]) * (1.0 + sc_ref[...]) + sh_ref[...]
    hb = h.astype(BF16)
    for n0 in range(0, N_PROJ_COLS, GROUP_WIDTH):
        n1 = min(n0 + GROUP_WIDTH, N_PROJ_COLS)
        res = _dot(hb, w_ref[:, n0:n1])
        for s in range((n1 - n0) // D_HEAD):
            o_ref[n0 // D_HEAD + s] = res[:, s * D_HEAD:(s + 1) * D_HEAD]


def _mod_spec(which, row_of_tile):
    return pl.BlockSpec((None, None, 1, D_MODEL), lambda i, *_: (row_of_tile(i), which, 0, 0))


def _in_projection(x, mod_l, norm_w, w_in, layer, row_of_tile, tm):
    n_tok = x.shape[0]
    return pl.pallas_call(
        _inproj_kernel,
        grid=(n_tok // tm,),
        in_specs=[
            pl.BlockSpec((tm, D_MODEL), lambda i: (i, 0)),
            _mod_spec(1, row_of_tile),
            _mod_spec(0, row_of_tile),
            pl.BlockSpec((1, D_MODEL), lambda i: (0, 0)),
            pl.BlockSpec((None, D_MODEL, N_PROJ_COLS), lambda i: (layer, 0, 0),
                         pipeline_mode=pl.Buffered(1)),
        ],
        out_specs=pl.BlockSpec((N_PROJ_COLS // D_HEAD, tm, D_HEAD), lambda i: (0, i, 0)),
        out_shape=jax.ShapeDtypeStruct((N_PROJ_COLS // D_HEAD, n_tok, D_HEAD), F32),
        compiler_params=_params("arbitrary"),
        name="norm_in_projection",
    )(x, mod_l, mod_l, norm_w, w_in)


def _hgrn_block(d, q, v, fz, lb, scan_ref, masks_ref, b_scr):
    c = HGRN_BLOCK
    e = jnp.exp(-jnp.abs(fz))
    inv = 1.0 / (1.0 + e)
    pos = fz >= 0.0
    k = (1.0 - lb) * (jnp.where(pos, e, 1.0) * inv)
    f = lb + (1.0 - lb) * (jnp.where(pos, 1.0, e) * inv)
    log2f = jnp.where(lb > 0.0, jnp.log2(f), jnp.minimum(fz, 0.0) * LOG2_E - jnp.log2(1.0 + e))

    b2 = _dot(scan_ref[d], _split_bf16(log2f))
    b = b2[:, :D_HEAD] + b2[:, D_HEAD:]
    total_row = c - 1 if d == 0 else 0
    total = b[total_row:total_row + 1]
    decay_in = jnp.exp2(b)
    decay_out = jnp.exp2(total - b)
    vb = v.astype(BF16)

    b_scr[...] = b

    scores = _dot_nt(q.astype(BF16), k.astype(BF16)).astype(BF16) * masks_ref[d, 0]
    t_idx = lax.broadcasted_iota(jnp.int32, (c, D_HEAD), 0)
    f_prev = pltpu.roll(f, 1, 0)
    f_next = pltpu.roll(f, c - 1, 0)
    for i, m in enumerate(LEVEL_HALVES):
        if m >= 8:
            parts = []
            for g in range(c // (2 * m)):
                ref_row = b_scr[g * 2 * m + (m - 1 if d == 0 else m):
                                g * 2 * m + (m - 1 if d == 0 else m) + 1, :]
                low = slice(g * 2 * m, g * 2 * m + m)
                high = slice(g * 2 * m + m, (g + 1) * 2 * m)
                if d == 0:
                    parts += [k[low] * jnp.exp2(ref_row - b[low]), q[high] * jnp.exp2(b[high] - ref_row)]
                else:
                    parts += [q[low] * jnp.exp2(b[low] - ref_row), k[high] * jnp.exp2(ref_row - b[high])]
            z = jnp.concatenate(parts, axis=0).astype(BF16)
            scores = scores + _dot_nt(z, z).astype(BF16) * masks_ref[d, 1 + i]
            continue
        if m == 4:
            parts = []
            for g in range(c // (2 * m)):
                r = g * 2 * m + (m - 1 if d == 0 else m)
                parts.append(jnp.exp2(-jnp.abs(b[g * 2 * m:(g + 1) * 2 * m] - b_scr[r:r + 1, :])))
            e_lvl = jnp.concatenate(parts, axis=0)
        elif m == 2:
            ph = t_idx % 4
            if d == 0:
                e_lvl = jnp.where(ph == 0, f_next, jnp.where(ph == 1, 1.0,
                                  jnp.where(ph == 2, f, f * f_prev)))
            else:
                e_lvl = jnp.where(ph == 0, f * f_next, jnp.where(ph == 1, f,
                                  jnp.where(ph == 2, 1.0, f_prev)))
        else:
            e_lvl = jnp.where(t_idx % 2 == (1 - d), f, 1.0)
        s_lvl = _dot_nt((q * e_lvl).astype(BF16), (k * e_lvl).astype(BF16))
        scores = scores + s_lvl.astype(BF16) * masks_ref[d, 1 + i]

    intra = _dot(scores, vb)
    q_in = (q * decay_in).astype(BF16)
    kv_t = _dot(v.T.astype(BF16), (k * decay_out).astype(BF16))
    return intra, q_in, kv_t, jnp.exp2(total)


def _hgrn_kernel(*refs, seq_len, heads, has_state, want_state):
    q_ref, ff_ref, fb_ref, v_ref, g_ref, lb_ref, nw_ref, scan_ref, masks_ref = refs[:9]
    pos = 9
    s0_ref = None
    if has_state:
        s0_ref = refs[pos]
        pos += 1
    o_ref = refs[pos]
    pos += 1
    sfin_ref = None
    if want_state:
        sfin_ref = refs[pos]
        pos += 1
    acc_ref, b_scr = refs[pos:pos + 2]
    n_blocks = seq_len // HGRN_BLOCK
    fz_refs = (ff_ref, fb_ref)
    chains = [(h, d) for h in range(heads) for d in range(2)]

    def scan_step(j, carry):
        states = list(carry)
        for i, (h, d) in enumerate(chains):
            rows = _block_rows(j if d == 0 else n_blocks - 1 - j, HGRN_BLOCK)
            intra, q_in, kv_t, decay = _hgrn_block(d, _silu(q_ref[h, rows, :]), v_ref[h, rows, :],
                                                   fz_refs[d][h, rows, :], lb_ref[d, h], scan_ref,
                                                   masks_ref, b_scr.at[h, d])
            acc_ref[h, d, rows, :] = intra + _dot_nt(q_in, states[i].astype(BF16))
            states[i] = states[i] * decay + kv_t
        return tuple(states)

    if has_state:
        init = tuple(s0_ref[d, h].T for h, d in chains)
    else:
        init = (jnp.zeros((D_HEAD, D_HEAD), F32),) * len(chains)
    final = lax.fori_loop(0, n_blocks, scan_step, init,
                          unroll=min(n_blocks, max(2, SCAN_UNROLL // heads)))
    if want_state:
        for i, (h, d) in enumerate(chains):
            sfin_ref[d, h] = final[i].T

    def finish(j, _):
        rows = _block_rows(j, HGRN_BLOCK)
        for h in range(heads):
            o = acc_ref[h, 0, rows, :] + acc_ref[h, 1, rows, :]
            o = o * lax.rsqrt(jnp.mean(o * o, axis=-1, keepdims=True) + EPS) * nw_ref[h]
            o_ref[h, rows, :] = (o * _silu(g_ref[h, rows, :])).astype(o_ref.dtype)
        return 0

    lax.fori_loop(0, n_blocks, finish, 0, unroll=min(n_blocks, max(2, 16 // heads)))


def _hgrn_mixer(proj, lb_l, norm_w, state0, batch, seq_len, want_state, tables, heads_per_step):
    scan, masks = tables
    has_state = state0 is not None
    n_tok = batch * seq_len

    hp = heads_per_step
    n_groups = N_HEADS // hp

    def col(section):
        return pl.BlockSpec((hp, seq_len, D_HEAD), lambda b, h: (section * n_groups + h, b, 0))

    in_specs = [col(0), col(1), col(2), col(3), col(4),
                pl.BlockSpec((2, hp, 1, D_HEAD), lambda b, h: (0, h, 0, 0)),
                pl.BlockSpec((hp, 1, D_HEAD), lambda b, h: (h, 0, 0)),
                pl.BlockSpec(scan.shape, lambda b, h: (0, 0, 0)),
                pl.BlockSpec(masks.shape, lambda b, h: (0, 0, 0, 0))]
    args = [proj, proj, proj, proj, proj, lb_l, norm_w, scan, masks]
    state_spec = pl.BlockSpec((None, 2, hp, D_HEAD, D_HEAD), lambda b, h: (b, 0, h, 0, 0))
    if has_state:
        in_specs.append(state_spec)
        args.append(state0)
    out_shape = [jax.ShapeDtypeStruct((N_HEADS, n_tok, D_HEAD), BF16)]
    out_specs = [pl.BlockSpec((hp, seq_len, D_HEAD), lambda b, h: (h, b, 0))]
    if want_state:
        out_shape.append(jax.ShapeDtypeStruct((batch, 2, N_HEADS, D_HEAD, D_HEAD), F32))
        out_specs.append(state_spec)
    res = pl.pallas_call(
        functools.partial(_hgrn_kernel, seq_len=seq_len, heads=hp, has_state=has_state,
                          want_state=want_state),
        grid=(batch, n_groups),
        in_specs=in_specs,
        out_specs=out_specs,
        out_shape=out_shape,
        scratch_shapes=[pltpu.VMEM((hp, 2, seq_len, D_HEAD), F32),
                        pltpu.VMEM((hp, 2, HGRN_BLOCK, D_HEAD), F32)],
        compiler_params=_params("arbitrary", "arbitrary"),
        name="hgrn2_mixer",
    )(*args)
    return res if want_state else (res[0], None)


def _dwconv_rows(x_ref, taps_ref, bias_ref, y_ref, seq_len, grid_w):
    w = grid_w
    n_rows = seq_len // w
    taps = taps_ref[...]
    bias = jnp.broadcast_to(bias_ref[...], (w, x_ref.shape[1]))
    col = lax.broadcasted_iota(jnp.int32, (w, x_ref.shape[1]), 0)
    y_ref[pl.ds(0, w), :] = jnp.zeros_like(bias)
    y_ref[pl.ds(w, w), :] = bias

    def grid_row(r, _):
        base = pl.multiple_of(r * w, w)
        x = x_ref[pl.ds(base, w), :]
        left = jnp.where(col == 0, 0.0, pltpu.roll(x, 1, 0))
        right = jnp.where(col == w - 1, 0.0, pltpu.roll(x, w - 1, 0))

        def kernel_row(i):
            return left * taps[3 * i] + x * taps[3 * i + 1] + right * taps[3 * i + 2]

        y_ref[pl.ds(base + 2 * w, w), :] = kernel_row(0) + bias
        y_ref[pl.ds(base + w, w), :] += kernel_row(1)
        y_ref[pl.ds(base, w), :] += kernel_row(2)
        return 0

    lax.fori_loop(0, n_rows, grid_row, 0, unroll=min(4, n_rows))


def _mlstm_gates(gates, sel):
    spread = _dot(_split_bf16(gates), sel)
    return [spread[:, i * D_HEAD:(i + 1) * D_HEAD] for i in range(sel.shape[1] // D_HEAD)]


def _mlstm_block(d, q, k, v, log_i, log_f, state, m_run, tri_ref, strict_ref, eye_ref):
    c = SCAN_BLOCK

    tri = tri_ref[d]
    vis = (strict_ref[d] + eye_ref[...]) > 0.5
    mix = log_f * strict_ref[d] + log_i * eye_ref[...]
    d2 = _dot(tri, _split_bf16(mix))
    d_log = d2[:, :c] + d2[:, c:]
    b2 = _dot(tri, _split_bf16(log_f))
    b_run = b2[:, :D_HEAD] + b2[:, D_HEAD:]

    inter_log = b_run + m_run
    d_max = jnp.max(jnp.where(vis, d_log, -jnp.inf), axis=1, keepdims=True)
    m_t = jnp.maximum(inter_log, jnp.broadcast_to(d_max, (c, D_HEAD)))
    p = jnp.where(vis, jnp.exp(d_log - m_t), 0.0)
    scores = _dot_nt(q.astype(BF16), k.astype(BF16)) * p
    v_aug = jnp.concatenate([v, jnp.ones((c, D_HEAD), F32)], axis=1).astype(BF16)
    intra = _dot(scores.astype(BF16), v_aug)
    inter = _dot(q.astype(BF16), state.astype(BF16))
    a_inter = jnp.exp(inter_log - m_t)
    num = intra[:, :D_HEAD] + a_inter * inter[:, :D_HEAD]
    den = intra[:, D_HEAD:] + a_inter * inter[:, D_HEAD:]
    out = num / jnp.maximum(jnp.abs(den), jnp.exp(-m_t))

    last_row = c - 1 if d == 0 else 0
    b_last = b_run[last_row:last_row + 1]
    w_end = b_last - b_run + log_i
    m_loc = jnp.max(w_end, axis=0, keepdims=True)
    p_end = jnp.exp(w_end - m_loc)
    kv = _dot((k * p_end).T.astype(BF16), v_aug)
    m_new = jnp.maximum(b_last + m_run, m_loc)
    a = jnp.exp(b_last + m_run - m_new)
    g = jnp.exp(m_loc - m_new)
    new_state = (jnp.concatenate([a, a], axis=1) * state + jnp.concatenate([g, g], axis=1) * kv)
    return out, new_state, m_new


def _mlstm_kernel(*refs, seq_len, grid_w, heads, has_state, want_state):
    (q_ref, k_ref, v_ref, og_ref, gates_ref, qtap_ref, ktap_ref, qb_ref, kb_ref, gb_ref, nw_ref,
     tri_ref, strict_ref, eye_ref, sel_ref) = refs[:15]
    pos = 15
    state_in = state_out = (None, None, None)
    if has_state:
        state_in = refs[pos:pos + 3]
        pos += 3
    o_ref = refs[pos]
    pos += 1
    if want_state:
        state_out = refs[pos:pos + 3]
        pos += 3
    scratch = refs[pos:pos + 5]
    for h in range(heads):
        _mlstm_head(q_ref.at[h], k_ref.at[h], v_ref.at[h], og_ref.at[h], gates_ref, qtap_ref.at[:, h],
                    ktap_ref.at[:, h], qb_ref.at[h], kb_ref.at[h], gb_ref, nw_ref.at[h], tri_ref,
                    strict_ref, eye_ref, sel_ref.at[h],
                    tuple(r.at[:, h] for r in state_in) if has_state else state_in,
                    o_ref.at[h],
                    tuple(r.at[:, h] for r in state_out) if want_state else state_out,
                    scratch, seq_len, grid_w)


def _mlstm_head(q_ref, k_ref, v_ref, og_ref, gates_ref, qtap_ref, ktap_ref, qb_ref, kb_ref, gb_ref,
                nw_ref, tri_ref, strict_ref, eye_ref, sel_ref, state_in, o_ref, state_out, scratch,
                seq_len, grid_w):
    c0_ref, n0_ref, m0_ref = state_in
    cfin_ref, nfin_ref, mfin_ref = state_out
    has_state = c0_ref is not None
    want_state = cfin_ref is not None
    yq_ref, yk_ref, accf_ref, accb_ref, gsp_ref = scratch

    n_blocks = seq_len // SCAN_BLOCK
    _dwconv_rows(q_ref, qtap_ref, qb_ref, yq_ref, seq_len, grid_w)
    _dwconv_rows(k_ref, ktap_ref, kb_ref, yk_ref, seq_len, grid_w)

    gate_bias = gb_ref[...]
    conv_align = int(np.gcd(grid_w, SCAN_BLOCK))

    def conv_rows(blk):
        return _block_rows(blk, SCAN_BLOCK, grid_w, conv_align)

    def activations(blk):
        cr = conv_rows(blk)
        return _silu(yq_ref[cr, :]), _silu(yk_ref[cr, :]) * (D_HEAD ** -0.5)

    prepared = n_blocks <= MLSTM_PREPARED_MAX_BLOCKS
    if prepared:
        for blk in range(n_blocks):
            rows, cr = _block_rows(blk), conv_rows(blk)
            yq_ref[cr, :], yk_ref[cr, :] = activations(blk)
            li_f, fz_f, li_b, fz_b = _mlstm_gates(gates_ref[rows, :] + gate_bias, sel_ref[...])
            for i, val in enumerate((li_f, _log_sigmoid(fz_f), li_b, _log_sigmoid(fz_b))):
                gsp_ref[rows, i * D_HEAD:(i + 1) * D_HEAD] = val

    def block_inputs(d, blk):
        rows, cr = _block_rows(blk), conv_rows(blk)
        if prepared:
            q, k = yq_ref[cr, :], yk_ref[cr, :]
            log_i = gsp_ref[rows, 2 * d * D_HEAD:(2 * d + 1) * D_HEAD]
            log_f = gsp_ref[rows, (2 * d + 1) * D_HEAD:(2 * d + 2) * D_HEAD]
        else:
            q, k = activations(blk)
            log_i, fz = _mlstm_gates(gates_ref[rows, :] + gate_bias,
                                     sel_ref[:, 2 * d * D_HEAD:(2 * d + 2) * D_HEAD])
            log_f = _log_sigmoid(fz)
        return q, k, v_ref[rows, :], log_i, log_f

    def scan_step(j, carry):
        st_f, m_f, st_b, m_b = carry
        blk_b = n_blocks - 1 - j
        out_f, st_f, m_f = _mlstm_block(0, *block_inputs(0, j), st_f, m_f, tri_ref, strict_ref,
                                        eye_ref)
        out_b, st_b, m_b = _mlstm_block(1, *block_inputs(1, blk_b), st_b, m_b, tri_ref, strict_ref,
                                        eye_ref)
        accf_ref[_block_rows(j), :] = out_f
        accb_ref[_block_rows(blk_b), :] = out_b
        return st_f, m_f, st_b, m_b

    init = []
    for d in range(2):
        if has_state:
            n_rep = jnp.broadcast_to(n0_ref[d], (D_HEAD, D_HEAD))
            init += [jnp.concatenate([c0_ref[d], n_rep], axis=1), m0_ref[d]]
        else:
            init += [jnp.zeros((D_HEAD, 2 * D_HEAD), F32), jnp.zeros((1, D_HEAD), F32)]
    fin = lax.fori_loop(0, n_blocks, scan_step, tuple(init), unroll=min(16, n_blocks))
    if want_state:
        for d in range(2):
            cfin_ref[d] = fin[2 * d][:, :D_HEAD]
            nfin_ref[d] = fin[2 * d][:, D_HEAD:D_HEAD + 1]
            mfin_ref[d] = fin[2 * d + 1]

    norm_w = nw_ref[...]

    def finish(j, _):
        rows = _block_rows(j)
        o = accf_ref[rows, :] + accb_ref[rows, :]
        o = o * lax.rsqrt(jnp.mean(o * o, axis=-1, keepdims=True) + EPS) * norm_w
        o_ref[rows, :] = (o * _sigmoid(og_ref[rows, :])).astype(o_ref.dtype)
        return 0

    lax.fori_loop(0, n_blocks, finish, 0, unroll=min(16, n_blocks))


def _mlstm_mixer(proj, conv_taps, conv_bias, gate_bias, norm_w, states0, batch, seq_len, grid_w,
                 want_state, tables, heads_per_step):
    tri, strict, eye, sel = tables
    has_state = states0 is not None
    n_tok = batch * seq_len

    hp = heads_per_step
    n_groups = N_HEADS // hp

    def col(section):
        return pl.BlockSpec((hp, seq_len, D_HEAD), lambda b, h: (section * n_groups + h, b, 0))

    def const(arr):
        return pl.BlockSpec(arr.shape, lambda b, h: (0,) * arr.ndim)

    def per_head(group_offset):
        return pl.BlockSpec((hp, 1, D_HEAD), lambda b, h: (group_offset + h, 0, 0))

    in_specs = [col(5), col(6), col(7), col(8),
                pl.BlockSpec((None, seq_len, D_HEAD), lambda b, h: (9 * N_HEADS, b, 0)),
                pl.BlockSpec((9, hp, 1, D_HEAD), lambda b, h: (0, h, 0, 0)),
                pl.BlockSpec((9, hp, 1, D_HEAD), lambda b, h: (0, n_groups + h, 0, 0)),
                per_head(0), per_head(n_groups), const(gate_bias), per_head(0),
                const(tri), const(strict), const(eye),
                pl.BlockSpec((hp,) + sel.shape[1:], lambda b, h: (h, 0, 0))]
    args = [proj, proj, proj, proj, proj, conv_taps, conv_taps, conv_bias, conv_bias, gate_bias,
            norm_w, tri, strict, eye, sel]

    def state_spec(rows, lanes):
        return pl.BlockSpec((None, 2, hp, rows, lanes), lambda b, h: (b, 0, h, 0, 0))

    state_specs = [state_spec(D_HEAD, D_HEAD), state_spec(D_HEAD, 1), state_spec(1, D_HEAD)]
    if has_state:
        in_specs += state_specs
        args += list(states0)
    out_shape = [jax.ShapeDtypeStruct((N_HEADS, n_tok, D_HEAD), BF16)]
    out_specs = [pl.BlockSpec((hp, seq_len, D_HEAD), lambda b, h: (h, b, 0))]
    if want_state:
        out_shape += [jax.ShapeDtypeStruct((batch, 2, N_HEADS, D_HEAD, D_HEAD), F32),
                      jax.ShapeDtypeStruct((batch, 2, N_HEADS, D_HEAD, 1), F32),
                      jax.ShapeDtypeStruct((batch, 2, N_HEADS, 1, D_HEAD), F32)]
        out_specs += state_specs
    res = pl.pallas_call(
        functools.partial(_mlstm_kernel, seq_len=seq_len, grid_w=grid_w, heads=hp,
                          has_state=has_state, want_state=want_state),
        grid=(batch, n_groups),
        in_specs=in_specs,
        out_specs=out_specs,
        out_shape=out_shape,
        scratch_shapes=[pltpu.VMEM((seq_len + 2 * grid_w, D_HEAD), F32)] * 2
        + [pltpu.VMEM((seq_len, D_HEAD), F32)] * 2
        + [pltpu.VMEM((seq_len if seq_len // SCAN_BLOCK <= MLSTM_PREPARED_MAX_BLOCKS
                       else SCAN_BLOCK, 4 * D_HEAD), F32)],
        compiler_params=_params("arbitrary", "arbitrary"),
        name="mlstm_mixer",
    )(*args)
    return (res[0], tuple(res[1:])) if want_state else (res[0], None)


def _post_kernel(x_ref, mh_ref, mm_ref, g1_ref, sc2_ref, sh2_ref, g2_ref, n2_ref, woh_ref, wom_ref,
                 wg_ref, wu_ref, wd_ref, fn_ref, o_ref, *, final_norm):
    mix_h = jnp.concatenate([mh_ref[h] for h in range(N_HEADS)], axis=1)
    mix_m = jnp.concatenate([mm_ref[h] for h in range(N_HEADS)], axis=1)
    mixed = _dot(mix_h, woh_ref[...]) + _dot(mix_m, wom_ref[...])
    x1 = x_ref[...] + g1_ref[...] * mixed
    hb = (_rmsnorm(x1, n2_ref[...]) * (1.0 + sc2_ref[...]) + sh2_ref[...]).astype(BF16)
    act = _silu(_dot(hb, wg_ref[...])) * _dot(hb, wu_ref[...])
    x2 = x1 + g2_ref[...] * _dot(act.astype(BF16), wd_ref[...])
    if final_norm:
        x2 = _rmsnorm(x2, fn_ref[...])
    o_ref[...] = x2


def _post_mixer(x, mix_h, mix_m, mod_l, norm2_w, w_out, w_gate, w_up, w_down, final_norm_w, layer,
                row_of_tile, tm, final_norm):
    n_tok = x.shape[0]
    row = lambda i: (i, 0)
    fixed = lambda i: (0, 0)

    def resident(shape, index_map):
        return pl.BlockSpec(shape, index_map, pipeline_mode=pl.Buffered(1))

    return pl.pallas_call(
        functools.partial(_post_kernel, final_norm=final_norm),
        grid=(n_tok // tm,),
        in_specs=[
            pl.BlockSpec((tm, D_MODEL), row),
            pl.BlockSpec((N_HEADS, tm, D_HEAD), lambda i: (0, i, 0)),
            pl.BlockSpec((N_HEADS, tm, D_HEAD), lambda i: (0, i, 0)),
            _mod_spec(2, row_of_tile), _mod_spec(4, row_of_tile), _mod_spec(3, row_of_tile),
            _mod_spec(5, row_of_tile),
            pl.BlockSpec((1, D_MODEL), fixed),
            resident((None, GROUP_WIDTH, D_MODEL), lambda i: (layer, 0, 0)),
            resident((None, GROUP_WIDTH, D_MODEL), lambda i: (layer, 1, 0)),
            resident((None, D_MODEL, D_FF), lambda i: (layer, 0, 0)),
            resident((None, D_MODEL, D_FF), lambda i: (layer, 0, 0)),
            resident((None, D_FF, D_MODEL), lambda i: (layer, 0, 0)),
            pl.BlockSpec((1, D_MODEL), fixed),
        ],
        out_specs=pl.BlockSpec((tm, D_MODEL), row),
        out_shape=jax.ShapeDtypeStruct((n_tok, D_MODEL), F32),
        compiler_params=_params("arbitrary"),
        name="out_projection_ffn",
    )(x, mix_h, mix_m, mod_l, mod_l, mod_l, mod_l, norm2_w, w_out, w_out, w_gate, w_up, w_down,
      final_norm_w)


def _cast_kernel(x_ref, o_ref):
    n_in = x_ref.shape[1]
    n_out = o_ref.shape[1]
    if n_out == n_in:
        o_ref[...] = x_ref[...].astype(o_ref.dtype)
    else:
        n_full = n_in - n_in % D_HEAD
        o_ref[:, :n_full] = x_ref[:, :n_full].astype(o_ref.dtype)
        o_ref[:, n_full:] = jnp.zeros((x_ref.shape[0], n_out - n_full), o_ref.dtype)
        o_ref[:, n_full:n_in] = x_ref[:, n_full:n_in].astype(o_ref.dtype)


def _to_bf16(w, n_out=None):
    depth, rows, cols = w.shape
    n_out = n_out or cols
    tr = int(np.gcd(rows, CAST_ROWS))
    return pl.pallas_call(
        _cast_kernel,
        grid=(depth, rows // tr),
        in_specs=[pl.BlockSpec((None, tr, cols), lambda l, i: (l, i, 0))],
        out_specs=pl.BlockSpec((None, tr, n_out), lambda l, i: (l, i, 0)),
        out_shape=jax.ShapeDtypeStruct((depth, rows, n_out), BF16),
        compiler_params=_params("arbitrary", "arbitrary"),
        name="weights_to_bf16",
    )(w)


def kernel(x_prompt, x_sample, state_hgrn, state_mlstm_c, state_mlstm_n, state_mlstm_m, c, c_ctx,
           norm1_w, norm2_w, w_mod, b_mod, w_in, conv_w, conv_b, ml_gate_b, hg_lb_logits,
           hg_norm_w, ml_norm_w, w_out, w_gate, w_up, w_down, final_norm_w):
    depth = w_in.shape[0]
    n_ctx, ctx_len, _ = x_prompt.shape
    n_lat, lat_len, _ = x_sample.shape
    assert 1 + n_lat <= N_MOD_ROWS and ctx_len % SCAN_BLOCK == 0 and lat_len % SCAN_BLOCK == 0

    hg_tables = _hgrn_tables()
    ml_tables = _mlstm_tables()
    lb_all = _lower_bounds(hg_lb_logits)
    cond = jnp.zeros((N_MOD_ROWS, D_MODEL), F32).at[0].set(c_ctx).at[1:1 + n_lat].set(c)
    mod = _modulation(cond, w_mod, b_mod).reshape(depth, N_MOD_ROWS, 6, 1, D_MODEL)

    tm_in = tm_post = TOKEN_TILE
    ctx_row_in = ctx_row_post = lambda i: 0
    lat_row_in = lambda i: 1 + i // (lat_len // tm_in)
    lat_row_post = lambda i: 1 + i // (lat_len // tm_post)

    xp = x_prompt.reshape(n_ctx * ctx_len, D_MODEL)
    xs = x_sample.reshape(n_lat * lat_len, D_MODEL)
    fin_w = final_norm_w.reshape(1, D_MODEL)
    hg_fin, mc_fin, mn_fin, mm_fin = [], [], [], []
    w_in_b = _to_bf16(w_in, N_PROJ_COLS)
    w_out_b, wg_b, wu_b, wd_b = _to_bf16(w_out), _to_bf16(w_gate), _to_bf16(w_up), _to_bf16(w_down)
    for l in range(depth):
        n1 = norm1_w[l].reshape(1, D_MODEL)
        n2 = norm2_w[l].reshape(1, D_MODEL)
        taps = conv_w[l].reshape(9, 2 * N_HEADS, 1, D_HEAD)
        cbias = conv_b[l].reshape(2 * N_HEADS, 1, D_HEAD)
        gbias = jnp.pad(ml_gate_b[l], (0, D_HEAD - N_GATE_COLS)).reshape(1, D_HEAD)
        hg_nw = hg_norm_w[l].reshape(N_HEADS, 1, D_HEAD)
        ml_nw = ml_norm_w[l].reshape(N_HEADS, 1, D_HEAD)
        last = l == depth - 1

        proj = _in_projection(xp, mod[l], n1, w_in_b, l, ctx_row_in, tm_in)
        mix_h, s_h = _hgrn_mixer(proj, lb_all[l], hg_nw, None, n_ctx, ctx_len, True, hg_tables,
                                 CTX_HEADS_PER_STEP)
        mix_m, s_m = _mlstm_mixer(proj, taps, cbias, gbias, ml_nw, None, n_ctx, ctx_len, ctx_len,
                                  True, ml_tables, CTX_HEADS_PER_STEP)
        xp = _post_mixer(xp, mix_h, mix_m, mod[l], n2, w_out_b, wg_b, wu_b, wd_b, fin_w, l,
                         ctx_row_post, tm_post, last)
        hg_fin.append(s_h)
        mc_fin.append(s_m[0])
        mn_fin.append(s_m[1][..., 0])
        mm_fin.append(s_m[2][..., 0, 0])

        cached = (state_mlstm_c[:, l].astype(F32), state_mlstm_n[:, l].astype(F32)[..., None],
                  jnp.broadcast_to(state_mlstm_m[:, l].astype(F32)[..., None, None],
                                   (n_lat, 2, N_HEADS, 1, D_HEAD)))
        proj = _in_projection(xs, mod[l], n1, w_in_b, l, lat_row_in, tm_in)
        mix_h, _ = _hgrn_mixer(proj, lb_all[l], hg_nw, state_hgrn[:, l].astype(F32), n_lat, lat_len,
                               False, hg_tables, LATENT_HEADS_PER_STEP)
        mix_m, _ = _mlstm_mixer(proj, taps, cbias, gbias, ml_nw, cached, n_lat, lat_len,
                                LATENT_GRID_W, False, ml_tables, LATENT_HEADS_PER_STEP)
        xs = _post_mixer(xs, mix_h, mix_m, mod[l], n2, w_out_b, wg_b, wu_b, wd_b, fin_w, l,
                         lat_row_post, tm_post, last)

    return (xp.reshape(x_prompt.shape), xs.reshape(x_sample.shape), jnp.stack(hg_fin, axis=1),
            jnp.stack(mc_fin, axis=1), jnp.stack(mn_fin, axis=1), jnp.stack(mm_fin, axis=1))
```
